```python
import math
import jax, jax.numpy as jnp
from jax import lax
import numpy as np

D_MODEL = 1024
BATCH = 16
SEQ = 2048
DEPTH = 4

CHUNK = 64
MIX_WIDTH = D_MODEL
A_WIDTH = MIX_WIDTH // 2
B_WIDTH = MIX_WIDTH - A_WIDTH
A_HEADS = 4
A_HEAD_DIM = A_WIDTH // A_HEADS
SPATIAL_CHUNK = 128
DIFF_HEADS = 4
DIFF_HEAD_DIM = B_WIDTH // (2 * DIFF_HEADS)
Q_BLOCK = 128
D_FF = 2816
CONV_WIDTH = 3
EPS = 1e-6
IN_WIDTH = 2 * A_WIDTH + 3 * B_WIDTH

kernel_name = "hybrid_gmlp_diffattn_convffn_trunk"


def rms_norm(x, w):
    xf = x.astype(jnp.float32)
    y = xf * lax.rsqrt(jnp.mean(xf * xf, axis=-1, keepdims=True) + EPS)
    return (y * w.astype(jnp.float32)).astype(x.dtype)


def chunk_causal_mask(q_pos, k_pos):
    return (k_pos[None, :] // CHUNK) <= (q_pos[:, None] // CHUNK)


def spatial_gating_unit(u, v, v_norm_w, w_s, b_s):
    bsz, seq, heads, dh = v.shape
    v = rms_norm(v, v_norm_w)
    n_chunks = seq // SPATIAL_CHUNK
    pos = jnp.arange(SPATIAL_CHUNK)
    mask = chunk_causal_mask(pos, pos)
    w = jnp.where(mask[None], w_s, jnp.zeros_like(w_s)).astype(v.dtype)
    vc = v.reshape(bsz, n_chunks, SPATIAL_CHUNK, heads, dh)
    mixed = jnp.einsum('hij,bnjhd->bnihd', w, vc) + b_s.T.astype(v.dtype)[None, None, :, :, None]
    return u * mixed.reshape(bsz, seq, heads, dh)


def diff_attention(q, k, v, lam):
    seq = q.shape[1]
    scale = DIFF_HEAD_DIM ** -0.5
    outs = []
    for qb in range(seq // Q_BLOCK):
        start, end = qb * Q_BLOCK, (qb + 1) * Q_BLOCK
        q_blk = q[:, start:end].astype(jnp.float32)
        k_blk = k[:, :end].astype(jnp.float32)
        s = jnp.einsum('bihcd,bjhcd->bhcij', q_blk, k_blk) * scale
        mask = chunk_causal_mask(jnp.arange(start, end), jnp.arange(end))
        s = jnp.where(mask, s, -jnp.inf)
        p = jax.nn.softmax(s, axis=-1)
        a = p[:, :, 0] - lam * p[:, :, 1]
        outs.append(jnp.einsum('bhij,bjhe->bihe', a, v[:, :end].astype(jnp.float32)))
    return jnp.concatenate(outs, axis=1).astype(v.dtype)


def causal_depthwise_conv(g, w, b):
    seq = g.shape[1]
    gp = jnp.pad(g, ((0, 0), (CONV_WIDTH - 1, 0), (0, 0)))
    out = b.astype(g.dtype)
    for tap in range(CONV_WIDTH):
        out = out + w[tap].astype(g.dtype) * gp[:, tap:tap + seq]
    return out


def hybrid_layer(x, layer_idx, norm_attn_w, w_in, gmlp_v_norm_w, spatial_w, spatial_b,
                 gmlp_out_norm_w, q_norm_w, k_norm_w, lambda_q1, lambda_k1, lambda_q2,
                 lambda_k2, diff_out_norm_w, w_out, norm_ffn_w, w_gate, w_up, conv_w,
                 conv_b, w_down):
    bsz, seq, _ = x.shape
    h = rms_norm(x, norm_attn_w)
    proj = h @ w_in
    splits = [A_WIDTH, 2 * A_WIDTH, 2 * A_WIDTH + B_WIDTH, 2 * A_WIDTH + 2 * B_WIDTH]
    a_u, a_v, b_q, b_k, b_v = jnp.split(proj, splits, axis=-1)

    a_u = jax.nn.gelu(a_u, approximate=False).reshape(bsz, seq, A_HEADS, A_HEAD_DIM)
    a_v = jax.nn.gelu(a_v, approximate=False).reshape(bsz, seq, A_HEADS, A_HEAD_DIM)
    a_out = spatial_gating_unit(a_u, a_v, gmlp_v_norm_w, spatial_w, spatial_b)
    a_out = rms_norm(a_out, gmlp_out_norm_w).reshape(bsz, seq, A_WIDTH)

    q = rms_norm(b_q.reshape(bsz, seq, DIFF_HEADS, 2, DIFF_HEAD_DIM), q_norm_w)
    k = rms_norm(b_k.reshape(bsz, seq, DIFF_HEADS, 2, DIFF_HEAD_DIM), k_norm_w)
    v = b_v.reshape(bsz, seq, DIFF_HEADS, 2 * DIFF_HEAD_DIM)
    lambda_init = 0.8 - 0.6 * math.exp(-0.3 * (layer_idx + 1))
    lam = (jnp.exp(jnp.sum(lambda_q1.astype(jnp.float32) * lambda_k1.astype(jnp.float32)))
           - jnp.exp(jnp.sum(lambda_q2.astype(jnp.float32) * lambda_k2.astype(jnp.float32)))
           + lambda_init)
    o = diff_attention(q, k, v, lam)
    b_out = (rms_norm(o, diff_out_norm_w) * (1.0 - lambda_init)).reshape(bsz, seq, B_WIDTH)

    x = x + jnp.concatenate([a_out, b_out], axis=-1) @ w_out

    h = rms_norm(x, norm_ffn_w)
    g = causal_depthwise_conv(h @ w_gate, conv_w, conv_b)
    f = jax.nn.silu(g) * (h @ w_up)
    return x + f @ w_down


def setup_inputs(seed: int = 0) -> dict:
    key = jax.random.key(seed)
    ks = jax.random.split(key, 20)
    f32 = jnp.float32
    L = DEPTH
    res_scale = (2 * DEPTH) ** -0.5

    def nrm(k, shape, scale):
        return jax.random.normal(k, shape, f32) * scale

    return {
        "x": nrm(ks[0], (BATCH, SEQ, D_MODEL), 1.0),
        "norm_attn_w": 1.0 + nrm(ks[1], (L, D_MODEL), 0.02),
        "w_in": nrm(ks[2], (L, D_MODEL, IN_WIDTH), D_MODEL ** -0.5),
        "gmlp_v_norm_w": 1.0 + nrm(ks[3], (L, A_HEADS, A_HEAD_DIM), 0.02),
        "spatial_w": nrm(ks[4], (L, A_HEADS, SPATIAL_CHUNK, SPATIAL_CHUNK), SPATIAL_CHUNK ** -0.5),
        "spatial_b": 1.0 + nrm(ks[5], (L, A_HEADS, SPATIAL_CHUNK), 0.02),
        "gmlp_out_norm_w": 1.0 + nrm(ks[6], (L, A_HEADS, A_HEAD_DIM), 0.02),
        "q_norm_w": 1.0 + nrm(ks[7], (L, DIFF_HEAD_DIM), 0.02),
        "k_norm_w": 1.0 + nrm(ks[8], (L, DIFF_HEAD_DIM), 0.02),
        "lambda_q1": nrm(ks[9], (L, DIFF_HEAD_DIM), 0.1),
        "lambda_k1": nrm(ks[10], (L, DIFF_HEAD_DIM), 0.1),
        "lambda_q2": nrm(ks[11], (L, DIFF_HEAD_DIM), 0.1),
        "lambda_k2": nrm(ks[12], (L, DIFF_HEAD_DIM), 0.1),
        "diff_out_norm_w": 1.0 + nrm(ks[13], (L, 2 * DIFF_HEAD_DIM), 0.02),
        "w_out": nrm(ks[14], (L, MIX_WIDTH, D_MODEL), MIX_WIDTH ** -0.5 * res_scale),
        "norm_ffn_w": 1.0 + nrm(ks[15], (L, D_MODEL), 0.02),
        "w_gate": nrm(ks[16], (L, D_MODEL, D_FF), D_MODEL ** -0.5),
        "w_up": nrm(ks[17], (L, D_MODEL, D_FF), D_MODEL ** -0.5),
        "conv_w": nrm(ks[18], (L, CONV_WIDTH, D_FF), CONV_WIDTH ** -0.5),
        "conv_b": nrm(jax.random.fold_in(ks[18], 1), (L, D_FF), 0.02),
        "w_down": nrm(ks[19], (L, D_FF, D_MODEL), D_FF ** -0.5 * res_scale),
    }


def reference(x, norm_attn_w, w_in, gmlp_v_norm_w, spatial_w, spatial_b, gmlp_out_norm_w,
              q_norm_w, k_norm_w, lambda_q1, lambda_k1, lambda_q2, lambda_k2,
              diff_out_norm_w, w_out, norm_ffn_w, w_gate, w_up, conv_w, conv_b, w_down):
    for layer in range(DEPTH):
        x = hybrid_layer(
            x, layer, norm_attn_w[layer], w_in[layer], gmlp_v_norm_w[layer], spatial_w[layer],
            spatial_b[layer], gmlp_out_norm_w[layer], q_norm_w[layer], k_norm_w[layer],
            lambda_q1[layer], lambda_k1[layer], lambda_q2[layer], lambda_k2[layer],
            diff_out_norm_w[layer], w_out[layer], norm_ffn_w[layer], w_gate[layer], w_up[layer],
            conv_w[layer], conv_b[layer], w_down[layer])
    return x
```

```python
import functools
import math

import jax
import jax.numpy as jnp
from jax import lax
from jax.experimental import pallas as pl
from jax.experimental.pallas import tpu as pltpu

D_MODEL = 1024
CHUNK = 64
A_WIDTH = 512
B_WIDTH = 512
A_HEADS = 4
A_HEAD_DIM = 128
SPATIAL_CHUNK = 128
DIFF_HEADS = 4
DIFF_HEAD_DIM = 64
V_DIM = 2 * DIFF_HEAD_DIM
D_FF = 2816
CONV_WIDTH = 3
EPS = 1e-6
IN_WIDTH = 2 * A_WIDTH + 3 * B_WIDTH

LANES = 128
SUBLANES = 8
VMEM_LIMIT_BYTES = 56 * 1024 * 1024

TOKEN_TILE = 512
Q_TILE = 256
K_TILE = 256
FF_CHUNKS = ((0, 1024), (1024, 1024), (2048, 768))

BF16 = jnp.bfloat16
F32 = jnp.float32


def _rms(x, w):
    return x * lax.rsqrt(jnp.mean(x * x, axis=-1, keepdims=True) + EPS) * w


def _gelu(x):
    return 0.5 * x * (1.0 + lax.erf(x * math.sqrt(0.5)))


def _half_norm(x, w, lo):
    sq = x * x
    s_all = jnp.sum(sq, axis=-1, keepdims=True)
    s_lo = jnp.sum(jnp.where(lo, sq, 0.0), axis=-1, keepdims=True)
    ms = jnp.where(lo, s_lo, s_all - s_lo) * (1.0 / DIFF_HEAD_DIM)
    return x * lax.rsqrt(ms + EPS) * w


def _mixer_in_kernel(x_ref, nw_ref, win_ref, vnw_ref, sw_ref, sbt_ref, onw_ref, qnw_ref, knw_ref,
                     a_ref, q_ref, k_ref, v_ref):
    tm = x_ref.shape[0]
    n_chunks = tm // SPATIAL_CHUNK
    h = _rms(x_ref[...], nw_ref[...]).astype(BF16)
    proj = jnp.dot(h, win_ref[...], preferred_element_type=F32)

    row = lax.broadcasted_iota(jnp.int32, (SPATIAL_CHUNK, SPATIAL_CHUNK), 0)
    col = lax.broadcasted_iota(jnp.int32, (SPATIAL_CHUNK, SPATIAL_CHUNK), 1)
    w_mask = (col // CHUNK) <= (row // CHUNK)
    for hd in range(A_HEADS):
        cs = slice(hd * A_HEAD_DIM, (hd + 1) * A_HEAD_DIM)
        u = _gelu(proj[:, hd * A_HEAD_DIM:(hd + 1) * A_HEAD_DIM])
        vv = _gelu(proj[:, A_WIDTH + hd * A_HEAD_DIM:A_WIDTH + (hd + 1) * A_HEAD_DIM])
        vn = _rms(vv, vnw_ref[:, cs]).astype(BF16)
        w_s = jnp.where(w_mask, sw_ref[hd], 0.0).astype(BF16)
        rhs = jnp.concatenate(
            [vn[c * SPATIAL_CHUNK:(c + 1) * SPATIAL_CHUNK] for c in range(n_chunks)], axis=1)
        mixed = jnp.dot(w_s, rhs, preferred_element_type=F32)
        b_col = sbt_ref[:, hd:hd + 1]
        mixed = jnp.concatenate(
            [mixed[:, c * SPATIAL_CHUNK:(c + 1) * SPATIAL_CHUNK] + b_col for c in range(n_chunks)],
            axis=0)
        a_ref[:, cs] = _rms(u * mixed, onw_ref[:, cs]).astype(BF16)

    lo = lax.broadcasted_iota(jnp.int32, (1, LANES), 1) < DIFF_HEAD_DIM
    scale = DIFF_HEAD_DIM ** -0.5
    for hd in range(DIFF_HEADS):
        cs = slice(hd * V_DIM, (hd + 1) * V_DIM)
        qh = proj[:, 2 * A_WIDTH + hd * V_DIM:2 * A_WIDTH + (hd + 1) * V_DIM]
        kh = proj[:, 2 * A_WIDTH + B_WIDTH + hd * V_DIM:2 * A_WIDTH + B_WIDTH + (hd + 1) * V_DIM]
        q_ref[:, cs] = (_half_norm(qh, qnw_ref[...], lo) * scale).astype(BF16)
        k_ref[:, cs] = _half_norm(kh, knw_ref[...], lo).astype(BF16)
    v_ref[...] = proj[:, 2 * A_WIDTH + 2 * B_WIDTH:].astype(BF16)


def _attn_kernel(lam_ref, q_ref, k_ref, v_ref, onw_ref, o_ref, *, out_scale):
    qi = pl.program_id(2)
    tq = q_ref.shape[1]
    q = q_ref[0]
    lane = lax.broadcasted_iota(jnp.int32, (1, LANES), 1)
    zero = jnp.zeros_like(q)
    q_subs = (jnp.where(lane < DIFF_HEAD_DIM, q, zero), jnp.where(lane >= DIFF_HEAD_DIM, q, zero))

    def step(k_blk, v_blk, state, mask):
        new_state = []
        for q_sub, (m, l, acc) in zip(q_subs, state):
            s = lax.dot_general(q_sub, k_blk, (((1,), (1,)), ((), ())), preferred_element_type=F32)
            if mask is not None:
                s = jnp.where(mask, s, -jnp.inf)
            m_new = jnp.maximum(m, jnp.max(s, axis=-1, keepdims=True))
            alpha = jnp.exp(m - m_new)
            p = jnp.exp(s - m_new)
            l_new = alpha * l + jnp.sum(p, axis=-1, keepdims=True)
            acc_new = alpha * acc + jnp.dot(p.astype(BF16), v_blk, preferred_element_type=F32)
            new_state.append((m_new, l_new, acc_new))
        return tuple(new_state)

    init = tuple((jnp.full((tq, 1), -jnp.inf, F32), jnp.zeros((tq, 1), F32),
                  jnp.zeros((tq, V_DIM), F32)) for _ in range(2))
    row = lax.broadcasted_iota(jnp.int32, (tq, K_TILE), 0)
    col = lax.broadcasted_iota(jnp.int32, (tq, K_TILE), 1)
    diag_mask = (col // CHUNK) <= (row // CHUNK)
    d0 = pl.multiple_of(qi * K_TILE, K_TILE)
    state = step(k_ref[0, pl.ds(d0, K_TILE), :], v_ref[0, pl.ds(d0, K_TILE), :], init, diag_mask)

    def body(j, st):
        k0 = pl.multiple_of(j * K_TILE, K_TILE)
        return step(k_ref[0, pl.ds(k0, K_TILE), :], v_ref[0, pl.ds(k0, K_TILE), :], st, None)

    (m1, l1, acc1), (m2, l2, acc2) = lax.fori_loop(0, qi, body, state)
    lam = lam_ref[0]
    o = acc1 / l1 - lam * (acc2 / l2)
    o_ref[0] = (_rms(o, onw_ref[...]) * out_scale).astype(BF16)


def _ffn_kernel(x_ref, a_ref, b_ref, wout_ref, nw_ref, wg_ref, wu_ref, cw_ref, cb_ref, wd_ref,
                o_ref, gs_ref, carry_ref, *, tiles_per_seq):
    tm = x_ref.shape[0]
    i = pl.program_id(0)

    @pl.when(i % tiles_per_seq == 0)
    def _():
        carry_ref[...] = jnp.zeros_like(carry_ref)

    x1 = (x_ref[...]
          + jnp.dot(a_ref[...], wout_ref[:A_WIDTH, :], preferred_element_type=F32)
          + jnp.dot(b_ref[...], wout_ref[A_WIDTH:, :], preferred_element_type=F32))
    h = _rms(x1, nw_ref[...]).astype(BF16)
    o_ref[...] = x1
    for c0, fc in FF_CHUNKS:
        g = jnp.dot(h, wg_ref[:, c0:c0 + fc], preferred_element_type=F32)
        gs_ref[0:SUBLANES, 0:fc] = carry_ref[:, c0:c0 + fc]
        gs_ref[SUBLANES:SUBLANES + tm, 0:fc] = g
        carry_ref[:, c0:c0 + fc] = g[tm - SUBLANES:, :]
        g1 = gs_ref[SUBLANES - 1:SUBLANES - 1 + tm, 0:fc]
        g2 = gs_ref[SUBLANES - 2:SUBLANES - 2 + tm, 0:fc]
        conv = (cb_ref[:, c0:c0 + fc] + cw_ref[0:1, c0:c0 + fc] * g2
                + cw_ref[1:2, c0:c0 + fc] * g1 + cw_ref[2:3, c0:c0 + fc] * g)
        up = jnp.dot(h, wu_ref[:, c0:c0 + fc], preferred_element_type=F32)
        f = (conv * jax.nn.sigmoid(conv) * up).astype(BF16)
        o_ref[...] += jnp.dot(f, wd_ref[c0:c0 + fc, :], preferred_element_type=F32)


def _full(shape):
    nd = len(shape)
    return pl.BlockSpec(shape, lambda *_: (0,) * nd, pipeline_mode=pl.Buffered(1))


def _mixer_in(x2, nw, win, vnw, sw, sbt, onw, qnw, knw):
    n = x2.shape[0]
    tm = TOKEN_TILE
    tok = lambda w: pl.BlockSpec((tm, w), lambda i: (i, 0))
    out = jax.ShapeDtypeStruct((n, B_WIDTH), BF16)
    return pl.pallas_call(
        _mixer_in_kernel,
        grid=(n // tm,),
        in_specs=[tok(D_MODEL), _full(nw.shape), _full(win.shape), _full(vnw.shape), _full(sw.shape),
                  _full(sbt.shape), _full(onw.shape), _full(qnw.shape), _full(knw.shape)],
        out_specs=[tok(A_WIDTH), tok(B_WIDTH), tok(B_WIDTH), tok(B_WIDTH)],
        out_shape=[out, out, out, out],
        compiler_params=pltpu.CompilerParams(
            dimension_semantics=("parallel",), vmem_limit_bytes=VMEM_LIMIT_BYTES),
        name="mixer_in",
    )(x2, nw, win, vnw, sw, sbt, onw, qnw, knw)


def _attn(lam, q, k, v, onw, out_scale):
    bsz, seq, _ = q.shape
    kv_spec = pl.BlockSpec((1, seq, V_DIM), lambda b, h, i: (b, 0, h))
    q_spec = pl.BlockSpec((1, Q_TILE, V_DIM), lambda b, h, i: (b, i, h))
    return pl.pallas_call(
        functools.partial(_attn_kernel, out_scale=out_scale),
        grid=(bsz, DIFF_HEADS, seq // Q_TILE),
        in_specs=[pl.BlockSpec(memory_space=pltpu.SMEM), q_spec, kv_spec, kv_spec,
                  pl.BlockSpec((1, V_DIM), lambda b, h, i: (0, 0))],
        out_specs=q_spec,
        out_shape=jax.ShapeDtypeStruct((bsz, seq, B_WIDTH), BF16),
        compiler_params=pltpu.CompilerParams(
            dimension_semantics=("parallel", "parallel", "parallel"),
            vmem_limit_bytes=VMEM_LIMIT_BYTES),
        name="attn",
    )(lam, q, k, v, onw)


def _ffn(x2, a, b, wout, nw, wg, wu, cw, cb, wd, seq):
    n = x2.shape[0]
    tm = TOKEN_TILE
    tok = lambda w: pl.BlockSpec((tm, w), lambda i: (i, 0))
    fc_max = max(fc for _, fc in FF_CHUNKS)
    return pl.pallas_call(
        functools.partial(_ffn_kernel, tiles_per_seq=seq // tm),
        grid=(n // tm,),
        in_specs=[tok(D_MODEL), tok(A_WIDTH), tok(B_WIDTH), _full(wout.shape), _full(nw.shape),
                  _full(wg.shape), _full(wu.shape), _full(cw.shape), _full(cb.shape), _full(wd.shape)],
        out_specs=tok(D_MODEL),
        out_shape=jax.ShapeDtypeStruct((n, D_MODEL), F32),
        scratch_shapes=[pltpu.VMEM((tm + SUBLANES, fc_max), F32), pltpu.VMEM((SUBLANES, D_FF), F32)],
        compiler_params=pltpu.CompilerParams(
            dimension_semantics=("arbitrary",), vmem_limit_bytes=VMEM_LIMIT_BYTES),
        name="ffn",
    )(x2, a, b, wout, nw, wg, wu, cw, cb, wd)


def kernel(x, norm_attn_w, w_in, gmlp_v_norm_w, spatial_w, spatial_b, gmlp_out_norm_w, q_norm_w, k_norm_w, lambda_q1, lambda_k1, lambda_q2, lambda_k2, diff_out_norm_w, w_out, norm_ffn_w, w_gate, w_up, conv_w, conv_b, w_down):
    bsz, seq, _ = x.shape
    depth = w_in.shape[0]
    assert seq % TOKEN_TILE == 0 and seq % Q_TILE == 0 and Q_TILE == K_TILE
    n = bsz * seq
    x2 = x.reshape(n, D_MODEL)
    for layer in range(depth):
        lambda_init = 0.8 - 0.6 * math.exp(-0.3 * (layer + 1))
        lam = (jnp.exp(jnp.sum(lambda_q1[layer] * lambda_k1[layer]))
               - jnp.exp(jnp.sum(lambda_q2[layer] * lambda_k2[layer])) + lambda_init).reshape(1)
        a, q, k, v = _mixer_in(
            x2, norm_attn_w[layer][None], w_in[layer].astype(BF16),
            gmlp_v_norm_w[layer].reshape(1, A_WIDTH), spatial_w[layer], spatial_b[layer].T,
            gmlp_out_norm_w[layer].reshape(1, A_WIDTH),
            jnp.tile(q_norm_w[layer], 2)[None], jnp.tile(k_norm_w[layer], 2)[None])
        b = _attn(lam, q.reshape(bsz, seq, B_WIDTH), k.reshape(bsz, seq, B_WIDTH),
                  v.reshape(bsz, seq, B_WIDTH), diff_out_norm_w[layer][None], 1.0 - lambda_init)
        x2 = _ffn(x2, a, b.reshape(n, B_WIDTH), w_out[layer].astype(BF16), norm_ffn_w[layer][None],
                  w_gate[layer].astype(BF16), w_up[layer].astype(BF16), conv_w[layer],
                  conv_b[layer][None], w_down[layer].astype(BF16), seq)
    return x2.reshape(bsz, seq, D_MODEL)
```

```python
import functools
import math

import jax
import jax.numpy as jnp
from jax import lax
from jax.experimental import pallas as pl
from jax.experimental.pallas import tpu as pltpu

D_MODEL = 1024
CHUNK = 64
A_WIDTH = 512
B_WIDTH = 512
A_HEADS = 4
A_HEAD_DIM = 128
SPATIAL_CHUNK = 128
DIFF_HEADS = 4
DIFF_HEAD_DIM = 64
V_DIM = 2 * DIFF_HEAD_DIM
D_FF = 2816
CONV_WIDTH = 3
EPS = 1e-6
LOG2_E = math.log2(math.e)
IN_WIDTH = 2 * A_WIDTH + 3 * B_WIDTH

LANES = 128
SUBLANES = 8
VMEM_LIMIT_BYTES = 56 * 1024 * 1024

TOKEN_TILE = 512
Q_TILE = 256
FF_CHUNKS = ((0, 1024), (1024, 1024), (2048, 768))

BF16 = jnp.bfloat16
F32 = jnp.float32


def _rms(x, w):
    return x * lax.rsqrt(jnp.mean(x * x, axis=-1, keepdims=True) + EPS) * w


def _gelu(x):
    return 0.5 * x * (1.0 + lax.erf(x * math.sqrt(0.5)))


def _half_norm(x, w, lo):
    sq = x * x
    s_all = jnp.sum(sq, axis=-1, keepdims=True)
    s_lo = jnp.sum(jnp.where(lo, sq, 0.0), axis=-1, keepdims=True)
    ms = jnp.where(lo, s_lo, s_all - s_lo) * (1.0 / DIFF_HEAD_DIM)
    return x * lax.rsqrt(ms + EPS) * w


def _mixer_in_kernel(x_ref, nw_ref, win_ref, vnw_ref, sw_ref, sbt_ref, onw_ref, qnw_ref, knw_ref,
                     a_ref, q_ref, k_ref, v_ref):
    tm = x_ref.shape[0]
    n_chunks = tm // SPATIAL_CHUNK
    h = _rms(x_ref[...], nw_ref[...]).astype(BF16)
    proj = jnp.dot(h, win_ref[...], preferred_element_type=F32)

    row = lax.broadcasted_iota(jnp.int32, (SPATIAL_CHUNK, SPATIAL_CHUNK), 0)
    col = lax.broadcasted_iota(jnp.int32, (SPATIAL_CHUNK, SPATIAL_CHUNK), 1)
    w_mask = (col // CHUNK) <= (row // CHUNK)
    for hd in range(A_HEADS):
        cs = slice(hd * A_HEAD_DIM, (hd + 1) * A_HEAD_DIM)
        u = _gelu(proj[:, hd * A_HEAD_DIM:(hd + 1) * A_HEAD_DIM])
        vv = _gelu(proj[:, A_WIDTH + hd * A_HEAD_DIM:A_WIDTH + (hd + 1) * A_HEAD_DIM])
        vn = _rms(vv, vnw_ref[:, cs]).astype(BF16)
        w_s = jnp.where(w_mask, sw_ref[hd], 0.0).astype(BF16)
        rhs = jnp.concatenate(
            [vn[c * SPATIAL_CHUNK:(c + 1) * SPATIAL_CHUNK] for c in range(n_chunks)], axis=1)
        mixed = jnp.dot(w_s, rhs, preferred_element_type=F32)
        b_col = sbt_ref[:, hd:hd + 1]
        mixed = jnp.concatenate(
            [mixed[:, c * SPATIAL_CHUNK:(c + 1) * SPATIAL_CHUNK] + b_col for c in range(n_chunks)],
            axis=0)
        a_ref[:, cs] = _rms(u * mixed, onw_ref[:, cs]).astype(BF16)

    lo = lax.broadcasted_iota(jnp.int32, (1, LANES), 1) < DIFF_HEAD_DIM
    scale = DIFF_HEAD_DIM ** -0.5 * LOG2_E
    for hd in range(DIFF_HEADS):
        cs = slice(hd * V_DIM, (hd + 1) * V_DIM)
        qh = proj[:, 2 * A_WIDTH + hd * V_DIM:2 * A_WIDTH + (hd + 1) * V_DIM]
        kh = proj[:, 2 * A_WIDTH + B_WIDTH + hd * V_DIM:2 * A_WIDTH + B_WIDTH + (hd + 1) * V_DIM]
        q_ref[:, cs] = (_half_norm(qh, qnw_ref[...], lo) * scale).astype(BF16)
        k_ref[:, cs] = _half_norm(kh, knw_ref[...], lo).astype(BF16)
    v_ref[...] = proj[:, 2 * A_WIDTH + 2 * B_WIDTH:].astype(BF16)


def _attn_kernel(lam_ref, q_ref, k_ref, v_ref, onw_ref, o_ref, *, out_scale):
    seq = q_ref.shape[1]
    lam = lam_ref[0]
    lane = lax.broadcasted_iota(jnp.int32, (1, LANES), 1)
    row = lax.broadcasted_iota(jnp.int32, (Q_TILE, Q_TILE), 0)
    col = lax.broadcasted_iota(jnp.int32, (Q_TILE, Q_TILE), 1)
    diag_mask = (col // CHUNK) <= (row // CHUNK)
    qk_dims = (((1,), (1,)), ((), ()))
    for t in range(seq // Q_TILE):
        q0 = t * Q_TILE
        q = q_ref[0, q0:q0 + Q_TILE, :]
        zero = jnp.zeros_like(q)
        k_diag = k_ref[0, q0:q0 + Q_TILE, :]
        probs = []
        for sub in range(2):
            keep = (lane < DIFF_HEAD_DIM) if sub == 0 else (lane >= DIFF_HEAD_DIM)
            q_sub = jnp.where(keep, q, zero)
            s_diag = lax.dot_general(q_sub, k_diag, qk_dims, preferred_element_type=F32)
            s_diag = jnp.where(diag_mask, s_diag, -jnp.inf)
            m = jnp.max(s_diag, axis=-1, keepdims=True)
            if t > 0:
                s_off = lax.dot_general(q_sub, k_ref[0, 0:q0, :], qk_dims, preferred_element_type=F32)
                m = jnp.maximum(m, jnp.max(s_off, axis=-1, keepdims=True))
                p_off = jnp.exp2(s_off - m)
            p_diag = jnp.exp2(s_diag - m)
            l = jnp.sum(p_diag, axis=-1, keepdims=True)
            if t > 0:
                l = l + jnp.sum(p_off, axis=-1, keepdims=True)
            c = (1.0 / l) if sub == 0 else (lam / l)
            probs.append((p_diag * c, (p_off * c) if t > 0 else None))
        (pd1, po1), (pd2, po2) = probs
        o = jnp.dot((pd1 - pd2).astype(BF16), v_ref[0, q0:q0 + Q_TILE, :], preferred_element_type=F32)
        if t > 0:
            o = o + jnp.dot((po1 - po2).astype(BF16), v_ref[0, 0:q0, :], preferred_element_type=F32)
        o_ref[0, q0:q0 + Q_TILE, :] = (_rms(o, onw_ref[...]) * out_scale).astype(BF16)


def _ffn_kernel(x_ref, a_ref, b_ref, wout_ref, nw_ref, wg_ref, wu_ref, cw_ref, cb_ref, wd_ref,
                o_ref, gs_ref, carry_ref, *, tiles_per_seq):
    tm = x_ref.shape[0]
    i = pl.program_id(0)

    @pl.when(i % tiles_per_seq == 0)
    def _():
        carry_ref[...] = jnp.zeros_like(carry_ref)

    x1 = (x_ref[...]
          + jnp.dot(a_ref[...], wout_ref[:A_WIDTH, :], preferred_element_type=F32)
          + jnp.dot(b_ref[...], wout_ref[A_WIDTH:, :], preferred_element_type=F32))
    h = _rms(x1, nw_ref[...]).astype(BF16)
    o_ref[...] = x1
    for c0, fc in FF_CHUNKS:
        g = jnp.dot(h, wg_ref[:, c0:c0 + fc], preferred_element_type=F32)
        gs_ref[0:SUBLANES, 0:fc] = carry_ref[:, c0:c0 + fc]
        gs_ref[SUBLANES:SUBLANES + tm, 0:fc] = g
        carry_ref[:, c0:c0 + fc] = g[tm - SUBLANES:, :]
        g1 = gs_ref[SUBLANES - 1:SUBLANES - 1 + tm, 0:fc]
        g2 = gs_ref[SUBLANES - 2:SUBLANES - 2 + tm, 0:fc]
        conv = (cb_ref[:, c0:c0 + fc] + cw_ref[0:1, c0:c0 + fc] * g2
                + cw_ref[1:2, c0:c0 + fc] * g1 + cw_ref[2:3, c0:c0 + fc] * g)
        up = jnp.dot(h, wu_ref[:, c0:c0 + fc], preferred_element_type=F32)
        f = (conv * jax.nn.sigmoid(conv) * up).astype(BF16)
        o_ref[...] += jnp.dot(f, wd_ref[c0:c0 + fc, :], preferred_element_type=F32)


def _full(shape):
    nd = len(shape)
    return pl.BlockSpec(shape, lambda *_: (0,) * nd, pipeline_mode=pl.Buffered(1))


def _mixer_in(x2, nw, win, vnw, sw, sbt, onw, qnw, knw):
    n = x2.shape[0]
    tm = TOKEN_TILE
    tok = lambda w: pl.BlockSpec((tm, w), lambda i: (i, 0))
    out = jax.ShapeDtypeStruct((n, B_WIDTH), BF16)
    return pl.pallas_call(
        _mixer_in_kernel,
        grid=(n // tm,),
        in_specs=[tok(D_MODEL), _full(nw.shape), _full(win.shape), _full(vnw.shape), _full(sw.shape),
                  _full(sbt.shape), _full(onw.shape), _full(qnw.shape), _full(knw.shape)],
        out_specs=[tok(A_WIDTH), tok(B_WIDTH), tok(B_WIDTH), tok(B_WIDTH)],
        out_shape=[out, out, out, out],
        compiler_params=pltpu.CompilerParams(
            dimension_semantics=("parallel",), vmem_limit_bytes=VMEM_LIMIT_BYTES),
        name="mixer_in",
    )(x2, nw, win, vnw, sw, sbt, onw, qnw, knw)


def _attn(lam, q, k, v, onw, out_scale):
    bsz, seq, _ = q.shape
    spec = pl.BlockSpec((1, seq, V_DIM), lambda b, h: (b, 0, h))
    return pl.pallas_call(
        functools.partial(_attn_kernel, out_scale=out_scale),
        grid=(bsz, DIFF_HEADS),
        in_specs=[pl.BlockSpec(memory_space=pltpu.SMEM), spec, spec, spec,
                  pl.BlockSpec((1, V_DIM), lambda b, h: (0, 0))],
        out_specs=spec,
        out_shape=jax.ShapeDtypeStruct((bsz, seq, B_WIDTH), BF16),
        compiler_params=pltpu.CompilerParams(
            dimension_semantics=("parallel", "parallel"), vmem_limit_bytes=VMEM_LIMIT_BYTES),
        name="attn",
    )(lam, q, k, v, onw)


def _ffn(x2, a, b, wout, nw, wg, wu, cw, cb, wd, seq):
    n = x2.shape[0]
    tm = TOKEN_TILE
    tok = lambda w: pl.BlockSpec((tm, w), lambda i: (i, 0))
    fc_max = max(fc for _, fc in FF_CHUNKS)
    return pl.pallas_call(
        functools.partial(_ffn_kernel, tiles_per_seq=seq // tm),
        grid=(n // tm,),
        in_specs=[tok(D_MODEL), tok(A_WIDTH), tok(B_WIDTH), _full(wout.shape), _full(nw.shape),
                  _full(wg.shape), _full(wu.shape), _full(cw.shape), _full(cb.shape), _full(wd.shape)],
        out_specs=tok(D_MODEL),
        out_shape=jax.ShapeDtypeStruct((n, D_MODEL), F32),
        scratch_shapes=[pltpu.VMEM((tm + SUBLANES, fc_max), F32), pltpu.VMEM((SUBLANES, D_FF), F32)],
        compiler_params=pltpu.CompilerParams(
            dimension_semantics=("arbitrary",), vmem_limit_bytes=VMEM_LIMIT_BYTES),
        name="ffn",
    )(x2, a, b, wout, nw, wg, wu, cw, cb, wd)


def kernel(x, norm_attn_w, w_in, gmlp_v_norm_w, spatial_w, spatial_b, gmlp_out_norm_w, q_norm_w, k_norm_w, lambda_q1, lambda_k1, lambda_q2, lambda_k2, diff_out_norm_w, w_out, norm_ffn_w, w_gate, w_up, conv_w, conv_b, w_down):
    bsz, seq, _ = x.shape
    depth = w_in.shape[0]
    assert seq % TOKEN_TILE == 0 and seq % Q_TILE == 0
    n = bsz * seq
    x2 = x.reshape(n, D_MODEL)
    for layer in range(depth):
        lambda_init = 0.8 - 0.6 * math.exp(-0.3 * (layer + 1))
        lam = (jnp.exp(jnp.sum(lambda_q1[layer] * lambda_k1[layer]))
               - jnp.exp(jnp.sum(lambda_q2[layer] * lambda_k2[layer])) + lambda_init).reshape(1)
        a, q, k, v = _mixer_in(
            x2, norm_attn_w[layer][None], w_in[layer].astype(BF16),
            gmlp_v_norm_w[layer].reshape(1, A_WIDTH), spatial_w[layer], spatial_b[layer].T,
            gmlp_out_norm_w[layer].reshape(1, A_WIDTH),
            jnp.tile(q_norm_w[layer], 2)[None], jnp.tile(k_norm_w[layer], 2)[None])
        b = _attn(lam, q.reshape(bsz, seq, B_WIDTH), k.reshape(bsz, seq, B_WIDTH),
                  v.reshape(bsz, seq, B_WIDTH), diff_out_norm_w[layer][None], 1.0 - lambda_init)
        x2 = _ffn(x2, a, b.reshape(n, B_WIDTH), w_out[layer].astype(BF16), norm_ffn_w[layer][None],
                  w_gate[layer].astype(BF16), w_up[layer].astype(BF16), conv_w[layer],
                  conv_b[layer][None], w_down[layer].astype(BF16), seq)
    return x2.reshape(bsz, seq, D_MODEL)
```

```python
import functools
import math

import jax
import jax.numpy as jnp
from jax import lax
from jax.experimental import pallas as pl
from jax.experimental.pallas import tpu as pltpu

D_MODEL = 1024
CHUNK = 64
A_WIDTH = 512
B_WIDTH = 512
A_HEADS = 4
A_HEAD_DIM = 128
SPATIAL_CHUNK = 128
DIFF_HEADS = 4
DIFF_HEAD_DIM = 64
V_DIM = 2 * DIFF_HEAD_DIM
D_FF = 2816
CONV_WIDTH = 3
EPS = 1e-6
LOG2_E = math.log2(math.e)
IN_WIDTH = 2 * A_WIDTH + 3 * B_WIDTH

LANES = 128
SUBLANES = 8
VMEM_LIMIT_BYTES = 56 * 1024 * 1024

TOKEN_TILE = 512
Q_TILE = 256
FF_CHUNKS = ((0, 1024), (1024, 1024), (2048, 768))

BF16 = jnp.bfloat16
F32 = jnp.float32


def _rms(x, w):
    return x * lax.rsqrt(jnp.mean(x * x, axis=-1, keepdims=True) + EPS) * w


def _gelu(x):
    return 0.5 * x * (1.0 + lax.erf(x * math.sqrt(0.5)))


def _half_norm(x, w, lo):
    sq = x * x
    s_all = jnp.sum(sq, axis=-1, keepdims=True)
    s_lo = jnp.sum(jnp.where(lo, sq, 0.0), axis=-1, keepdims=True)
    ms = jnp.where(lo, s_lo, s_all - s_lo) * (1.0 / DIFF_HEAD_DIM)
    return x * lax.rsqrt(ms + EPS) * w


def _mixer_in_kernel(x_ref, nw_ref, win_ref, vnw_ref, sw_ref, sbt_ref, onw_ref, qnw_ref, knw_ref,
                     a_ref, q_ref, k_ref, v_ref):
    tm = x_ref.shape[0]
    n_chunks = tm // SPATIAL_CHUNK
    h = _rms(x_ref[...], nw_ref[...]).astype(BF16)
    proj = jnp.dot(h, win_ref[...], preferred_element_type=F32)

    row = lax.broadcasted_iota(jnp.int32, (SPATIAL_CHUNK, SPATIAL_CHUNK), 0)
    col = lax.broadcasted_iota(jnp.int32, (SPATIAL_CHUNK, SPATIAL_CHUNK), 1)
    w_mask = (col // CHUNK) <= (row // CHUNK)
    for hd in range(A_HEADS):
        cs = slice(hd * A_HEAD_DIM, (hd + 1) * A_HEAD_DIM)
        u = _gelu(proj[:, hd * A_HEAD_DIM:(hd + 1) * A_HEAD_DIM])
        vv = _gelu(proj[:, A_WIDTH + hd * A_HEAD_DIM:A_WIDTH + (hd + 1) * A_HEAD_DIM])
        vn = _rms(vv, vnw_ref[:, cs]).astype(BF16)
        w_s = jnp.where(w_mask, sw_ref[hd], 0.0).astype(BF16)
        rhs = jnp.concatenate(
            [vn[c * SPATIAL_CHUNK:(c + 1) * SPATIAL_CHUNK] for c in range(n_chunks)], axis=1)
        mixed = jnp.dot(w_s, rhs, preferred_element_type=F32)
        b_col = sbt_ref[:, hd:hd + 1]
        mixed = jnp.concatenate(
            [mixed[:, c * SPATIAL_CHUNK:(c + 1) * SPATIAL_CHUNK] + b_col for c in range(n_chunks)],
            axis=0)
        a_ref[:, cs] = _rms(u * mixed, onw_ref[:, cs]).astype(BF16)

    lo = lax.broadcasted_iota(jnp.int32, (1, LANES), 1) < DIFF_HEAD_DIM
    scale = DIFF_HEAD_DIM ** -0.5 * LOG2_E
    for hd in range(DIFF_HEADS):
        cs = slice(hd * V_DIM, (hd + 1) * V_DIM)
        qh = proj[:, 2 * A_WIDTH + hd * V_DIM:2 * A_WIDTH + (hd + 1) * V_DIM]
        kh = proj[:, 2 * A_WIDTH + B_WIDTH + hd * V_DIM:2 * A_WIDTH + B_WIDTH + (hd + 1) * V_DIM]
        q_ref[:, cs] = (_half_norm(qh, qnw_ref[...], lo) * scale).astype(BF16)
        k_ref[:, cs] = _half_norm(kh, knw_ref[...], lo).astype(BF16)
    v_ref[...] = proj[:, 2 * A_WIDTH + 2 * B_WIDTH:].astype(BF16)


def _interleave(tasks_a, tasks_b):
    na, nb = len(tasks_a), len(tasks_b)
    ia = ib = 0
    while ia < na or ib < nb:
        if ib >= nb or (ia < na and (ia + 1) * nb <= (ib + 1) * na):
            tasks_a[ia]()
            ia += 1
        else:
            tasks_b[ib]()
            ib += 1


def _attn_kernel(lam_ref, q_ref, k_ref, v_ref, onw_ref, o_ref, s_ref, mx_ref, ls_ref, a_ref, *, out_scale):
    seq = q_ref.shape[1]
    n_tiles = seq // Q_TILE
    lam = lam_ref[0]
    lane = lax.broadcasted_iota(jnp.int32, (1, LANES), 1)
    keeps = (lane < DIFF_HEAD_DIM, lane >= DIFF_HEAD_DIM)
    row = lax.broadcasted_iota(jnp.int32, (Q_TILE, Q_TILE), 0)
    col = lax.broadcasted_iota(jnp.int32, (Q_TILE, Q_TILE), 1)
    diag_mask = (col // CHUNK) <= (row // CHUNK)
    qk_dims = (((1,), (1,)), ((), ()))

    def rows(ref, u):
        return ref[0, u * Q_TILE:(u + 1) * Q_TILE, :]

    def cols(u):
        return slice(u * Q_TILE, (u + 1) * Q_TILE)

    def fold(x):
        return x[:, :LANES], x[:, LANES:]

    def score_tasks(t):
        par = t % 2
        q = rows(q_ref, t)
        zero = jnp.zeros_like(q)
        q_subs = [jnp.where(keep, q, zero) for keep in keeps]

        def task(sub, u):
            def run():
                s = lax.dot_general(q_subs[sub], rows(k_ref, u), qk_dims, preferred_element_type=F32)
                if u == t:
                    s = jnp.where(diag_mask, s, -jnp.inf)
                s_ref[par, sub, :, cols(u)] = s
                lo, hi = fold(s)
                mx = jnp.maximum(lo, hi)
                mx_ref[par, sub] = mx if u == 0 else jnp.maximum(mx_ref[par, sub], mx)
            return run
        return [task(sub, u) for u in range(t + 1) for sub in range(2)]

    def softmax_tasks(t, st):
        par = t % 2
        n = t + 1

        def exp_task(sub, u):
            def run():
                if u == 0:
                    st["m"][sub] = jnp.max(mx_ref[par, sub], axis=-1, keepdims=True)
                p = jnp.exp2(s_ref[par, sub, :, cols(u)] - st["m"][sub])
                s_ref[par, sub, :, cols(u)] = p
                lo, hi = fold(p)
                ls_ref[sub] = (lo + hi) if u == 0 else ls_ref[sub] + (lo + hi)
                if u == n - 1:
                    st["l"][sub] = jnp.sum(ls_ref[sub], axis=-1, keepdims=True)
            return run

        def mix_task(u):
            def run():
                if u == 0:
                    st["r"] = lam * st["l"][0] / st["l"][1]
                a_ref[:, cols(u)] = (s_ref[par, 0, :, cols(u)]
                                     - st["r"] * s_ref[par, 1, :, cols(u)]).astype(BF16)
            return run

        return ([exp_task(sub, u) for u in range(n) for sub in range(2)]
                + [mix_task(u) for u in range(n)])

    for task in score_tasks(0):
        task()
    for t in range(n_tiles):
        n = t + 1
        st = {"m": [None, None], "l": [None, None], "r": None}
        _interleave(score_tasks(t + 1) if t + 1 < n_tiles else [], softmax_tasks(t, st))
        o = jnp.dot(a_ref[:, 0:n * Q_TILE], v_ref[0, 0:n * Q_TILE, :], preferred_element_type=F32)
        o = o * (1.0 / st["l"][0])
        o_ref[0, t * Q_TILE:(t + 1) * Q_TILE, :] = (_rms(o, onw_ref[...]) * out_scale).astype(BF16)


def _ffn_kernel(x_ref, a_ref, b_ref, wout_ref, nw_ref, wg_ref, wu_ref, cw_ref, cb_ref, wd_ref,
                o_ref, gs_ref, carry_ref, *, tiles_per_seq):
    tm = x_ref.shape[0]
    i = pl.program_id(0)

    @pl.when(i % tiles_per_seq == 0)
    def _():
        carry_ref[...] = jnp.zeros_like(carry_ref)

    x1 = (x_ref[...]
          + jnp.dot(a_ref[...], wout_ref[:A_WIDTH, :], preferred_element_type=F32)
          + jnp.dot(b_ref[...], wout_ref[A_WIDTH:, :], preferred_element_type=F32))
    h = _rms(x1, nw_ref[...]).astype(BF16)
    o_ref[...] = x1
    for c0, fc in FF_CHUNKS:
        g = jnp.dot(h, wg_ref[:, c0:c0 + fc], preferred_element_type=F32)
        gs_ref[0:SUBLANES, 0:fc] = carry_ref[:, c0:c0 + fc]
        gs_ref[SUBLANES:SUBLANES + tm, 0:fc] = g
        carry_ref[:, c0:c0 + fc] = g[tm - SUBLANES:, :]
        g1 = gs_ref[SUBLANES - 1:SUBLANES - 1 + tm, 0:fc]
        g2 = gs_ref[SUBLANES - 2:SUBLANES - 2 + tm, 0:fc]
        conv = (cb_ref[:, c0:c0 + fc] + cw_ref[0:1, c0:c0 + fc] * g2
                + cw_ref[1:2, c0:c0 + fc] * g1 + cw_ref[2:3, c0:c0 + fc] * g)
        up = jnp.dot(h, wu_ref[:, c0:c0 + fc], preferred_element_type=F32)
        f = (conv * jax.nn.sigmoid(conv) * up).astype(BF16)
        o_ref[...] += jnp.dot(f, wd_ref[c0:c0 + fc, :], preferred_element_type=F32)


def _full(shape):
    nd = len(shape)
    return pl.BlockSpec(shape, lambda *_: (0,) * nd, pipeline_mode=pl.Buffered(1))


def _mixer_in(x2, nw, win, vnw, sw, sbt, onw, qnw, knw):
    n = x2.shape[0]
    tm = TOKEN_TILE
    tok = lambda w: pl.BlockSpec((tm, w), lambda i: (i, 0))
    out = jax.ShapeDtypeStruct((n, B_WIDTH), BF16)
    return pl.pallas_call(
        _mixer_in_kernel,
        grid=(n // tm,),
        in_specs=[tok(D_MODEL), _full(nw.shape), _full(win.shape), _full(vnw.shape), _full(sw.shape),
                  _full(sbt.shape), _full(onw.shape), _full(qnw.shape), _full(knw.shape)],
        out_specs=[tok(A_WIDTH), tok(B_WIDTH), tok(B_WIDTH), tok(B_WIDTH)],
        out_shape=[out, out, out, out],
        compiler_params=pltpu.CompilerParams(
            dimension_semantics=("parallel",), vmem_limit_bytes=VMEM_LIMIT_BYTES),
        name="mixer_in",
    )(x2, nw, win, vnw, sw, sbt, onw, qnw, knw)


def _attn(lam, q, k, v, onw, out_scale):
    bsz, seq, _ = q.shape
    spec = pl.BlockSpec((1, seq, V_DIM), lambda b, h: (b, 0, h))
    return pl.pallas_call(
        functools.partial(_attn_kernel, out_scale=out_scale),
        grid=(bsz, DIFF_HEADS),
        in_specs=[pl.BlockSpec(memory_space=pltpu.SMEM), spec, spec, spec,
                  pl.BlockSpec((1, V_DIM), lambda b, h: (0, 0))],
        out_specs=spec,
        out_shape=jax.ShapeDtypeStruct((bsz, seq, B_WIDTH), BF16),
        scratch_shapes=[pltpu.VMEM((2, 2, Q_TILE, seq), F32),
                        pltpu.VMEM((2, 2, Q_TILE, LANES), F32),
                        pltpu.VMEM((2, Q_TILE, LANES), F32),
                        pltpu.VMEM((Q_TILE, seq), BF16)],
        compiler_params=pltpu.CompilerParams(
            dimension_semantics=("parallel", "parallel"), vmem_limit_bytes=VMEM_LIMIT_BYTES),
        name="attn",
    )(lam, q, k, v, onw)


def _ffn(x2, a, b, wout, nw, wg, wu, cw, cb, wd, seq):
    n = x2.shape[0]
    tm = TOKEN_TILE
    tok = lambda w: pl.BlockSpec((tm, w), lambda i: (i, 0))
    fc_max = max(fc for _, fc in FF_CHUNKS)
    return pl.pallas_call(
        functools.partial(_ffn_kernel, tiles_per_seq=seq // tm),
        grid=(n // tm,),
        in_specs=[tok(D_MODEL), tok(A_WIDTH), tok(B_WIDTH), _full(wout.shape), _full(nw.shape),
                  _full(wg.shape), _full(wu.shape), _full(cw.shape), _full(cb.shape), _full(wd.shape)],
        out_specs=tok(D_MODEL),
        out_shape=jax.ShapeDtypeStruct((n, D_MODEL), F32),
        scratch_shapes=[pltpu.VMEM((tm + SUBLANES, fc_max), F32), pltpu.VMEM((SUBLANES, D_FF), F32)],
        compiler_params=pltpu.CompilerParams(
            dimension_semantics=("arbitrary",), vmem_limit_bytes=VMEM_LIMIT_BYTES),
        name="ffn",
    )(x2, a, b, wout, nw, wg, wu, cw, cb, wd)


def kernel(x, norm_attn_w, w_in, gmlp_v_norm_w, spatial_w, spatial_b, gmlp_out_norm_w, q_norm_w, k_norm_w, lambda_q1, lambda_k1, lambda_q2, lambda_k2, diff_out_norm_w, w_out, norm_ffn_w, w_gate, w_up, conv_w, conv_b, w_down):
    bsz, seq, _ = x.shape
    depth = w_in.shape[0]
    assert seq % TOKEN_TILE == 0 and seq % Q_TILE == 0
    n = bsz * seq
    x2 = x.reshape(n, D_MODEL)
    for layer in range(depth):
        lambda_init = 0.8 - 0.6 * math.exp(-0.3 * (layer + 1))
        lam = (jnp.exp(jnp.sum(lambda_q1[layer] * lambda_k1[layer]))
               - jnp.exp(jnp.sum(lambda_q2[layer] * lambda_k2[layer])) + lambda_init).reshape(1)
        a, q, k, v = _mixer_in(
            x2, norm_attn_w[layer][None], w_in[layer].astype(BF16),
            gmlp_v_norm_w[layer].reshape(1, A_WIDTH), spatial_w[layer], spatial_b[layer].T,
            gmlp_out_norm_w[layer].reshape(1, A_WIDTH),
            jnp.tile(q_norm_w[layer], 2)[None], jnp.tile(k_norm_w[layer], 2)[None])
        b = _attn(lam, q.reshape(bsz, seq, B_WIDTH), k.reshape(bsz, seq, B_WIDTH),
                  v.reshape(bsz, seq, B_WIDTH), diff_out_norm_w[layer][None], 1.0 - lambda_init)
        x2 = _ffn(x2, a, b.reshape(n, B_WIDTH), w_out[layer].astype(BF16), norm_ffn_w[layer][None],
                  w_gate[layer].astype(BF16), w_up[layer].astype(BF16), conv_w[layer],
                  conv_b[layer][None], w_down[layer].astype(BF16), seq)
    return x2.reshape(bsz, seq, D_MODEL)
```

```python
import functools
import math

import jax
import jax.numpy as jnp
from jax import lax
from jax.experimental import pallas as pl
from jax.experimental.pallas import tpu as pltpu

D_MODEL = 1024
CHUNK = 64
A_WIDTH = 512
B_WIDTH = 512
A_HEADS = 4
A_HEAD_DIM = 128
SPATIAL_CHUNK = 128
DIFF_HEADS = 4
DIFF_HEAD_DIM = 64
V_DIM = 2 * DIFF_HEAD_DIM
D_FF = 2816
CONV_WIDTH = 3
EPS = 1e-6
LOG2_E = math.log2(math.e)
IN_WIDTH = 2 * A_WIDTH + 3 * B_WIDTH

LANES = 128
SUBLANES = 8
VMEM_LIMIT_BYTES = 56 * 1024 * 1024

TOKEN_TILE = 512
Q_TILE = 256

BF16 = jnp.bfloat16
F32 = jnp.float32


def _rms(x, w):
    return x * lax.rsqrt(jnp.mean(x * x, axis=-1, keepdims=True) + EPS) * w


def _gelu(x):
    return 0.5 * x * (1.0 + lax.erf(x * math.sqrt(0.5)))


def _half_norm(x, w, lo):
    sq = x * x
    s_all = jnp.sum(sq, axis=-1, keepdims=True)
    s_lo = jnp.sum(jnp.where(lo, sq, 0.0), axis=-1, keepdims=True)
    ms = jnp.where(lo, s_lo, s_all - s_lo) * (1.0 / DIFF_HEAD_DIM)
    return x * lax.rsqrt(ms + EPS) * w


def _mixer_in_kernel(x_ref, nw_ref, win_ref, vnw_ref, sw_ref, sbt_ref, onw_ref, qnw_ref, knw_ref,
                     a_ref, q_ref, k_ref, v_ref):
    tm = x_ref.shape[0]
    n_chunks = tm // SPATIAL_CHUNK
    h = _rms(x_ref[...], nw_ref[...]).astype(BF16)
    proj = jnp.dot(h, win_ref[...], preferred_element_type=F32)

    row = lax.broadcasted_iota(jnp.int32, (SPATIAL_CHUNK, SPATIAL_CHUNK), 0)
    col = lax.broadcasted_iota(jnp.int32, (SPATIAL_CHUNK, SPATIAL_CHUNK), 1)
    w_mask = (col // CHUNK) <= (row // CHUNK)
    for hd in range(A_HEADS):
        cs = slice(hd * A_HEAD_DIM, (hd + 1) * A_HEAD_DIM)
        u = _gelu(proj[:, hd * A_HEAD_DIM:(hd + 1) * A_HEAD_DIM])
        vv = _gelu(proj[:, A_WIDTH + hd * A_HEAD_DIM:A_WIDTH + (hd + 1) * A_HEAD_DIM])
        vn = _rms(vv, vnw_ref[:, cs]).astype(BF16)
        w_s = jnp.where(w_mask, sw_ref[hd], 0.0).astype(BF16)
        rhs = jnp.concatenate(
            [vn[c * SPATIAL_CHUNK:(c + 1) * SPATIAL_CHUNK] for c in range(n_chunks)], axis=1)
        mixed = jnp.dot(w_s, rhs, preferred_element_type=F32)
        b_col = sbt_ref[:, hd:hd + 1]
        mixed = jnp.concatenate(
            [mixed[:, c * SPATIAL_CHUNK:(c + 1) * SPATIAL_CHUNK] + b_col for c in range(n_chunks)],
            axis=0)
        a_ref[:, cs] = _rms(u * mixed, onw_ref[:, cs]).astype(BF16)

    lo = lax.broadcasted_iota(jnp.int32, (1, LANES), 1) < DIFF_HEAD_DIM
    scale = DIFF_HEAD_DIM ** -0.5 * LOG2_E
    for hd in range(DIFF_HEADS):
        cs = slice(hd * V_DIM, (hd + 1) * V_DIM)
        qh = proj[:, 2 * A_WIDTH + hd * V_DIM:2 * A_WIDTH + (hd + 1) * V_DIM]
        kh = proj[:, 2 * A_WIDTH + B_WIDTH + hd * V_DIM:2 * A_WIDTH + B_WIDTH + (hd + 1) * V_DIM]
        q_ref[:, cs] = (_half_norm(qh, qnw_ref[...], lo) * scale).astype(BF16)
        k_ref[:, cs] = _half_norm(kh, knw_ref[...], lo).astype(BF16)
    v_ref[...] = proj[:, 2 * A_WIDTH + 2 * B_WIDTH:].astype(BF16)


def _interleave(tasks_a, tasks_b):
    na, nb = len(tasks_a), len(tasks_b)
    ia = ib = 0
    while ia < na or ib < nb:
        if ib >= nb or (ia < na and (ia + 1) * nb <= (ib + 1) * na):
            tasks_a[ia]()
            ia += 1
        else:
            tasks_b[ib]()
            ib += 1


def _attn_kernel(lam_ref, q_ref, k_ref, v_ref, onw_ref, o_ref, s_ref, a_ref, vt_ref, *, out_scale):
    seq = q_ref.shape[1]
    n_tiles = seq // Q_TILE
    lam = lam_ref[0]
    lane = lax.broadcasted_iota(jnp.int32, (1, LANES), 1)
    keeps = (lane < DIFF_HEAD_DIM, lane >= DIFF_HEAD_DIM)
    key = lax.broadcasted_iota(jnp.int32, (Q_TILE, Q_TILE), 0)
    qry = lax.broadcasted_iota(jnp.int32, (Q_TILE, Q_TILE), 1)
    diag_mask = (key // CHUNK) <= (qry // CHUNK)
    qk_dims = (((1,), (1,)), ((), ()))

    def rows(u):
        return slice(u * Q_TILE, (u + 1) * Q_TILE)

    vt_ref[...] = v_ref[0].T

    def score_tasks(t, st):
        par = t % 2
        q = q_ref[0, rows(t), :]
        zero = jnp.zeros_like(q)
        q_subs = [jnp.where(keep, q, zero) for keep in keeps]

        def task(sub, u):
            def run():
                s = lax.dot_general(k_ref[0, rows(u), :], q_subs[sub], qk_dims, preferred_element_type=F32)
                if u == t:
                    s = jnp.where(diag_mask, s, -jnp.inf)
                s_ref[par, sub, rows(u), :] = s
                mx = jnp.max(s.reshape(Q_TILE // SUBLANES, SUBLANES, Q_TILE), axis=0)
                st["mx"][sub] = mx if u == 0 else jnp.maximum(st["mx"][sub], mx)
            return run
        return [task(sub, u) for u in range(t + 1) for sub in range(2)]

    def softmax_tasks(t, st):
        par = t % 2
        n = t + 1

        def exp_task(sub, u):
            def run():
                if u == 0:
                    st["m"][sub] = jnp.max(st["mx"][sub], axis=0, keepdims=True)
                p = jnp.exp2(s_ref[par, sub, rows(u), :] - st["m"][sub])
                s_ref[par, sub, rows(u), :] = p
                ls = jnp.sum(p.reshape(Q_TILE // SUBLANES, SUBLANES, Q_TILE), axis=0)
                st["ls"][sub] = ls if u == 0 else st["ls"][sub] + ls
                if u == n - 1:
                    st["l"][sub] = jnp.sum(st["ls"][sub], axis=0, keepdims=True)
            return run

        def mix_task(u):
            def run():
                if u == 0:
                    st["r"] = lam * st["l"][0] / st["l"][1]
                a_ref[rows(u), :] = (s_ref[par, 0, rows(u), :]
                                     - st["r"] * s_ref[par, 1, rows(u), :]).astype(BF16)
            return run

        return ([exp_task(sub, u) for u in range(n) for sub in range(2)]
                + [mix_task(u) for u in range(n)])

    st_next = {"mx": [None, None], "m": [None, None], "ls": [None, None], "l": [None, None], "r": None}
    for task in score_tasks(0, st_next):
        task()
    for t in range(n_tiles):
        n = t + 1
        st = st_next
        st_next = {"mx": [None, None], "m": [None, None], "ls": [None, None], "l": [None, None], "r": None}
        _interleave(score_tasks(t + 1, st_next) if t + 1 < n_tiles else [], softmax_tasks(t, st))
        ot = jnp.dot(vt_ref[:, 0:n * Q_TILE], a_ref[0:n * Q_TILE, :], preferred_element_type=F32)
        ot = ot * (1.0 / st["l"][0])
        ot = ot * lax.rsqrt(jnp.mean(ot * ot, axis=0, keepdims=True) + EPS)
        o_ref[0, rows(t), :] = (ot.T * onw_ref[...] * out_scale).astype(BF16)


def _ffn_kernel(x_ref, a_ref, b_ref, wout_ref, nw_ref, wg_ref, wu_ref, cw_ref, cb_ref, wd_ref,
                o_ref, gs_ref, carry_ref, *, tiles_per_seq):
    tm = x_ref.shape[0]
    i = pl.program_id(0)

    @pl.when(i % tiles_per_seq == 0)
    def _():
        carry_ref[...] = jnp.zeros_like(carry_ref)

    half = tm // 2
    hs = []
    for r0 in (0, half):
        rs = slice(r0, r0 + half)
        x1 = (x_ref[rs, :]
              + jnp.dot(a_ref[rs, :], wout_ref[:A_WIDTH, :], preferred_element_type=F32)
              + jnp.dot(b_ref[rs, :], wout_ref[A_WIDTH:, :], preferred_element_type=F32))
        o_ref[rs, :] = x1
        hs.append(_rms(x1, nw_ref[...]).astype(BF16))
    gs_ref[0:SUBLANES, :] = carry_ref[...]
    for r0, h in zip((0, half), hs):
        gs_ref[SUBLANES + r0:SUBLANES + r0 + half, :] = jnp.dot(h, wg_ref[...], preferred_element_type=F32)
    carry_ref[...] = gs_ref[tm:tm + SUBLANES, :]
    g = gs_ref[SUBLANES:SUBLANES + tm, :]
    g1 = gs_ref[SUBLANES - 1:SUBLANES - 1 + tm, :]
    g2 = gs_ref[SUBLANES - 2:SUBLANES - 2 + tm, :]
    conv = cb_ref[...] + cw_ref[0:1, :] * g2 + cw_ref[1:2, :] * g1 + cw_ref[2:3, :] * g
    up = jnp.dot(jnp.concatenate(hs, axis=0), wu_ref[...], preferred_element_type=F32)
    f = (conv * jax.nn.sigmoid(conv) * up).astype(BF16)
    o_ref[...] += jnp.dot(f, wd_ref[...], preferred_element_type=F32)


def _full(shape):
    nd = len(shape)
    return pl.BlockSpec(shape, lambda *_: (0,) * nd, pipeline_mode=pl.Buffered(1))


def _mixer_in(x2, nw, win, vnw, sw, sbt, onw, qnw, knw):
    n = x2.shape[0]
    tm = TOKEN_TILE
    tok = lambda w: pl.BlockSpec((tm, w), lambda i: (i, 0))
    out = jax.ShapeDtypeStruct((n, B_WIDTH), BF16)
    return pl.pallas_call(
        _mixer_in_kernel,
        grid=(n // tm,),
        in_specs=[tok(D_MODEL), _full(nw.shape), _full(win.shape), _full(vnw.shape), _full(sw.shape),
                  _full(sbt.shape), _full(onw.shape), _full(qnw.shape), _full(knw.shape)],
        out_specs=[tok(A_WIDTH), tok(B_WIDTH), tok(B_WIDTH), tok(B_WIDTH)],
        out_shape=[out, out, out, out],
        compiler_params=pltpu.CompilerParams(
            dimension_semantics=("parallel",), vmem_limit_bytes=VMEM_LIMIT_BYTES),
        name="mixer_in",
    )(x2, nw, win, vnw, sw, sbt, onw, qnw, knw)


def _attn(lam, q, k, v, onw, out_scale):
    bsz, seq, _ = q.shape
    spec = pl.BlockSpec((1, seq, V_DIM), lambda b, h: (b, 0, h))
    return pl.pallas_call(
        functools.partial(_attn_kernel, out_scale=out_scale),
        grid=(bsz, DIFF_HEADS),
        in_specs=[pl.BlockSpec(memory_space=pltpu.SMEM), spec, spec, spec,
                  pl.BlockSpec((1, V_DIM), lambda b, h: (0, 0))],
        out_specs=spec,
        out_shape=jax.ShapeDtypeStruct((bsz, seq, B_WIDTH), BF16),
        scratch_shapes=[pltpu.VMEM((2, 2, seq, Q_TILE), F32),
                        pltpu.VMEM((seq, Q_TILE), BF16),
                        pltpu.VMEM((V_DIM, seq), BF16)],
        compiler_params=pltpu.CompilerParams(
            dimension_semantics=("parallel", "parallel"), vmem_limit_bytes=VMEM_LIMIT_BYTES),
        name="attn",
    )(lam, q, k, v, onw)


def _ffn(x2, a, b, wout, nw, wg, wu, cw, cb, wd, seq):
    n = x2.shape[0]
    tm = TOKEN_TILE
    tok = lambda w: pl.BlockSpec((tm, w), lambda i: (i, 0))
    return pl.pallas_call(
        functools.partial(_ffn_kernel, tiles_per_seq=seq // tm),
        grid=(n // tm,),
        in_specs=[tok(D_MODEL), tok(A_WIDTH), tok(B_WIDTH), _full(wout.shape), _full(nw.shape),
                  _full(wg.shape), _full(wu.shape), _full(cw.shape), _full(cb.shape), _full(wd.shape)],
        out_specs=tok(D_MODEL),
        out_shape=jax.ShapeDtypeStruct((n, D_MODEL), F32),
        scratch_shapes=[pltpu.VMEM((tm + SUBLANES, D_FF), F32), pltpu.VMEM((SUBLANES, D_FF), F32)],
        compiler_params=pltpu.CompilerParams(
            dimension_semantics=("arbitrary",), vmem_limit_bytes=VMEM_LIMIT_BYTES),
        name="ffn",
    )(x2, a, b, wout, nw, wg, wu, cw, cb, wd)


def kernel(x, norm_attn_w, w_in, gmlp_v_norm_w, spatial_w, spatial_b, gmlp_out_norm_w, q_norm_w, k_norm_w, lambda_q1, lambda_k1, lambda_q2, lambda_k2, diff_out_norm_w, w_out, norm_ffn_w, w_gate, w_up, conv_w, conv_b, w_down):
    bsz, seq, _ = x.shape
    depth = w_in.shape[0]
    assert seq % TOKEN_TILE == 0 and seq % Q_TILE == 0
    n = bsz * seq
    x2 = x.reshape(n, D_MODEL)
    for layer in range(depth):
        lambda_init = 0.8 - 0.6 * math.exp(-0.3 * (layer + 1))
        lam = (jnp.exp(jnp.sum(lambda_q1[layer] * lambda_k1[layer]))
               - jnp.exp(jnp.sum(lambda_q2[layer] * lambda_k2[layer])) + lambda_init).reshape(1)
        a, q, k, v = _mixer_in(
            x2, norm_attn_w[layer][None], w_in[layer].astype(BF16),
            gmlp_v_norm_w[layer].reshape(1, A_WIDTH), spatial_w[layer], spatial_b[layer].T,
            gmlp_out_norm_w[layer].reshape(1, A_WIDTH),
            jnp.tile(q_norm_w[layer], 2)[None], jnp.tile(k_norm_w[layer], 2)[None])
        b = _attn(lam, q.reshape(bsz, seq, B_WIDTH), k.reshape(bsz, seq, B_WIDTH),
                  v.reshape(bsz, seq, B_WIDTH), diff_out_norm_w[layer][None], 1.0 - lambda_init)
        x2 = _ffn(x2, a, b.reshape(n, B_WIDTH), w_out[layer].astype(BF16), norm_ffn_w[layer][None],
                  w_gate[layer].astype(BF16), w_up[layer].astype(BF16), conv_w[layer],
                  conv_b[layer][None], w_down[layer].astype(BF16), seq)
    return x2.reshape(bsz, seq, D_MODEL)
```

```python
import functools
import math

import jax
import jax.numpy as jnp
from jax import lax
from jax.experimental import pallas as pl
from jax.experimental.pallas import tpu as pltpu

D_MODEL = 1024
CHUNK = 64
A_WIDTH = 512
B_WIDTH = 512
A_HEADS = 4
A_HEAD_DIM = 128
SPATIAL_CHUNK = 128
DIFF_HEADS = 4
DIFF_HEAD_DIM = 64
V_DIM = 2 * DIFF_HEAD_DIM
D_FF = 2816
CONV_WIDTH = 3
EPS = 1e-6
LOG2_E = math.log2(math.e)
IN_WIDTH = 2 * A_WIDTH + 3 * B_WIDTH

LANES = 128
SUBLANES = 8
VMEM_LIMIT_BYTES = 56 * 1024 * 1024

MIXER_TILE = 1024
TOKEN_TILE = 512
Q_TILE = 256

BF16 = jnp.bfloat16
F32 = jnp.float32


def _rms(x, w):
    return x * lax.rsqrt(jnp.mean(x * x, axis=-1, keepdims=True) + EPS) * w


def _gelu(x):
    return 0.5 * x * (1.0 + lax.erf(x * math.sqrt(0.5)))


def _half_norm(x, w, lo):
    sq = x * x
    s_all = jnp.sum(sq, axis=-1, keepdims=True)
    s_lo = jnp.sum(jnp.where(lo, sq, 0.0), axis=-1, keepdims=True)
    ms = jnp.where(lo, s_lo, s_all - s_lo) * (1.0 / DIFF_HEAD_DIM)
    return x * lax.rsqrt(ms + EPS) * w


def _mixer_in_kernel(x_ref, nw_ref, win_ref, vnw_ref, sw_ref, sbt_ref, onw_ref, qnw_ref, knw_ref,
                     a_ref, q_ref, k_ref, v_ref):
    tm = x_ref.shape[0]
    n_chunks = tm // SPATIAL_CHUNK
    h = _rms(x_ref[...], nw_ref[...]).astype(BF16)
    proj = jnp.dot(h, win_ref[...], preferred_element_type=F32)

    row = lax.broadcasted_iota(jnp.int32, (SPATIAL_CHUNK, SPATIAL_CHUNK), 0)
    col = lax.broadcasted_iota(jnp.int32, (SPATIAL_CHUNK, SPATIAL_CHUNK), 1)
    w_mask = (col // CHUNK) <= (row // CHUNK)
    for hd in range(A_HEADS):
        cs = slice(hd * A_HEAD_DIM, (hd + 1) * A_HEAD_DIM)
        u = _gelu(proj[:, hd * A_HEAD_DIM:(hd + 1) * A_HEAD_DIM])
        vv = _gelu(proj[:, A_WIDTH + hd * A_HEAD_DIM:A_WIDTH + (hd + 1) * A_HEAD_DIM])
        vn = _rms(vv, vnw_ref[:, cs]).astype(BF16)
        w_s = jnp.where(w_mask, sw_ref[hd], 0.0).astype(BF16)
        rhs = jnp.concatenate(
            [vn[c * SPATIAL_CHUNK:(c + 1) * SPATIAL_CHUNK] for c in range(n_chunks)], axis=1)
        mixed = jnp.dot(w_s, rhs, preferred_element_type=F32)
        b_col = sbt_ref[:, hd:hd + 1]
        mixed = jnp.concatenate(
            [mixed[:, c * SPATIAL_CHUNK:(c + 1) * SPATIAL_CHUNK] + b_col for c in range(n_chunks)],
            axis=0)
        a_ref[:, cs] = _rms(u * mixed, onw_ref[:, cs]).astype(BF16)

    lo = lax.broadcasted_iota(jnp.int32, (1, LANES), 1) < DIFF_HEAD_DIM
    scale = DIFF_HEAD_DIM ** -0.5 * LOG2_E
    for hd in range(DIFF_HEADS):
        cs = slice(hd * V_DIM, (hd + 1) * V_DIM)
        qh = proj[:, 2 * A_WIDTH + hd * V_DIM:2 * A_WIDTH + (hd + 1) * V_DIM]
        kh = proj[:, 2 * A_WIDTH + B_WIDTH + hd * V_DIM:2 * A_WIDTH + B_WIDTH + (hd + 1) * V_DIM]
        q_ref[:, cs] = (_half_norm(qh, qnw_ref[...], lo) * scale).astype(BF16)
        k_ref[:, cs] = _half_norm(kh, knw_ref[...], lo).astype(BF16)
    v_ref[...] = proj[:, 2 * A_WIDTH + 2 * B_WIDTH:].astype(BF16)


def _interleave(tasks_a, tasks_b):
    na, nb = len(tasks_a), len(tasks_b)
    ia = ib = 0
    while ia < na or ib < nb:
        if ib >= nb or (ia < na and (ia + 1) * nb <= (ib + 1) * na):
            tasks_a[ia]()
            ia += 1
        else:
            tasks_b[ib]()
            ib += 1


def _attn_kernel(lam_ref, q_ref, k_ref, v_ref, onw_ref, o_ref, s_ref, a_ref, vt_ref, *, layer, out_scale):
    seq = q_ref.shape[1]
    n_tiles = seq // Q_TILE
    lam = lam_ref[layer]
    lane = lax.broadcasted_iota(jnp.int32, (1, LANES), 1)
    keeps = (lane < DIFF_HEAD_DIM, lane >= DIFF_HEAD_DIM)
    qk_dims = (((1,), (1,)), ((), ()))
    band = Q_TILE // 2

    def rows(u):
        return slice(u * Q_TILE, (u + 1) * Q_TILE)

    def pieces(t, u):
        if u < t:
            return [(rows(u), slice(0, Q_TILE), None)]

        def mask(width):
            key = lax.broadcasted_iota(jnp.int32, (band, width), 0)
            qry = lax.broadcasted_iota(jnp.int32, (band, width), 1)
            return (key // CHUNK) <= (qry // CHUNK)
        k0 = u * Q_TILE
        return [(slice(k0, k0 + band), slice(0, Q_TILE), mask(Q_TILE)),
                (slice(k0 + band, k0 + Q_TILE), slice(band, Q_TILE), mask(band))]

    def widen(x, cs, fill):
        if cs.start == 0:
            return x
        return jnp.concatenate([jnp.full((x.shape[0], cs.start), fill, x.dtype), x], axis=1)

    def partial_rows(x):
        return x.reshape(x.shape[0] // SUBLANES, SUBLANES, x.shape[1])

    vt_ref[...] = v_ref[0].T

    def score_tasks(t, st):
        par = t % 2
        q = q_ref[0, rows(t), :]
        zero = jnp.zeros_like(q)
        q_subs = [jnp.where(keep, q, zero) for keep in keeps]

        def task(sub, u):
            def run():
                for ks, cs, mask in pieces(t, u):
                    s = lax.dot_general(k_ref[0, ks, :], q_subs[sub][cs, :], qk_dims,
                                        preferred_element_type=F32)
                    if mask is not None:
                        s = jnp.where(mask, s, -jnp.inf)
                    s_ref[par, sub, ks, cs] = s
                    mx = widen(jnp.max(partial_rows(s), axis=0), cs, -jnp.inf)
                    st["mx"][sub] = mx if st["mx"][sub] is None else jnp.maximum(st["mx"][sub], mx)
            return run
        return [task(sub, u) for u in range(t + 1) for sub in range(2)]

    def softmax_tasks(t, st):
        par = t % 2
        n = t + 1

        def exp_task(sub, u):
            def run():
                for ks, cs, _ in pieces(t, u):
                    m = jnp.max(st["mx"][sub][:, cs], axis=0, keepdims=True)
                    p = jnp.exp2(s_ref[par, sub, ks, cs] - m)
                    s_ref[par, sub, ks, cs] = p
                    ls = widen(jnp.sum(partial_rows(p), axis=0), cs, 0.0)
                    st["ls"][sub] = ls if st["ls"][sub] is None else st["ls"][sub] + ls
            return run

        def mix_task(u):
            def run():
                for ks, cs, _ in pieces(t, u):
                    l1, l2 = (jnp.sum(st["ls"][sub][:, cs], axis=0, keepdims=True) for sub in range(2))
                    if cs.start:
                        a_ref[ks, 0:cs.start] = jnp.zeros((ks.stop - ks.start, cs.start), BF16)
                    a_ref[ks, cs] = (s_ref[par, 0, ks, cs]
                                     - (lam * l1 / l2) * s_ref[par, 1, ks, cs]).astype(BF16)
            return run

        return ([exp_task(sub, u) for u in range(n) for sub in range(2)]
                + [mix_task(u) for u in range(n)])

    def new_stats():
        return {"mx": [None, None], "ls": [None, None]}

    st_next = new_stats()
    for task in score_tasks(0, st_next):
        task()
    for t in range(n_tiles):
        n = t + 1
        st = st_next
        st_next = new_stats()
        _interleave(score_tasks(t + 1, st_next) if t + 1 < n_tiles else [], softmax_tasks(t, st))
        ot = jnp.dot(vt_ref[:, 0:n * Q_TILE], a_ref[0:n * Q_TILE, :], preferred_element_type=F32)
        ot = ot * (1.0 / jnp.sum(st["ls"][0], axis=0, keepdims=True))
        ot = ot * lax.rsqrt(jnp.mean(ot * ot, axis=0, keepdims=True) + EPS)
        o_ref[0, rows(t), :] = (ot.T * onw_ref[...] * out_scale).astype(BF16)


def _ffn_kernel(x_ref, a_ref, b_ref, wout_ref, nw_ref, wg_ref, wu_ref, cw_ref, cb_ref, wd_ref,
                o_ref, gs_ref, carry_ref, *, tiles_per_seq):
    tm = x_ref.shape[0]
    i = pl.program_id(0)

    @pl.when(i % tiles_per_seq == 0)
    def _():
        carry_ref[...] = jnp.zeros_like(carry_ref)

    half = tm // 2
    hs = []
    for r0 in (0, half):
        rs = slice(r0, r0 + half)
        x1 = (x_ref[rs, :]
              + jnp.dot(a_ref[rs, :], wout_ref[:A_WIDTH, :], preferred_element_type=F32)
              + jnp.dot(b_ref[rs, :], wout_ref[A_WIDTH:, :], preferred_element_type=F32))
        o_ref[rs, :] = x1
        hs.append(_rms(x1, nw_ref[...]).astype(BF16))
    gs_ref[0:SUBLANES, :] = carry_ref[...]
    for r0, h in zip((0, half), hs):
        gs_ref[SUBLANES + r0:SUBLANES + r0 + half, :] = jnp.dot(h, wg_ref[...], preferred_element_type=F32)
    carry_ref[...] = gs_ref[tm:tm + SUBLANES, :]
    g = gs_ref[SUBLANES:SUBLANES + tm, :]
    g1 = gs_ref[SUBLANES - 1:SUBLANES - 1 + tm, :]
    g2 = gs_ref[SUBLANES - 2:SUBLANES - 2 + tm, :]
    conv = cb_ref[...] + cw_ref[0:1, :] * g2 + cw_ref[1:2, :] * g1 + cw_ref[2:3, :] * g
    up = jnp.dot(jnp.concatenate(hs, axis=0), wu_ref[...], preferred_element_type=F32)
    f = (conv * jax.nn.sigmoid(conv) * up).astype(BF16)
    o_ref[...] += jnp.dot(f, wd_ref[...], preferred_element_type=F32)


def _layer_spec(arr, layer):
    nd = arr.ndim - 1
    return pl.BlockSpec((None,) + arr.shape[1:], lambda *_: (layer,) + (0,) * nd,
                        pipeline_mode=pl.Buffered(1))


def _mixer_in(x2, layer, params):
    n = x2.shape[0]
    tm = MIXER_TILE
    tok = lambda w: pl.BlockSpec((tm, w), lambda i: (i, 0))
    out = jax.ShapeDtypeStruct((n, B_WIDTH), BF16)
    return pl.pallas_call(
        _mixer_in_kernel,
        grid=(n // tm,),
        in_specs=[tok(D_MODEL)] + [_layer_spec(p, layer) for p in params],
        out_specs=[tok(A_WIDTH), tok(B_WIDTH), tok(B_WIDTH), tok(B_WIDTH)],
        out_shape=[out, out, out, out],
        compiler_params=pltpu.CompilerParams(
            dimension_semantics=("parallel",), vmem_limit_bytes=VMEM_LIMIT_BYTES),
        name="mixer_in",
    )(x2, *params)


def _attn(lam, q, k, v, onw, layer, out_scale):
    bsz, seq, _ = q.shape
    spec = pl.BlockSpec((1, seq, V_DIM), lambda b, h: (b, 0, h))
    return pl.pallas_call(
        functools.partial(_attn_kernel, layer=layer, out_scale=out_scale),
        grid=(bsz, DIFF_HEADS),
        in_specs=[pl.BlockSpec(memory_space=pltpu.SMEM), spec, spec, spec, _layer_spec(onw, layer)],
        out_specs=spec,
        out_shape=jax.ShapeDtypeStruct((bsz, seq, B_WIDTH), BF16),
        scratch_shapes=[pltpu.VMEM((2, 2, seq, Q_TILE), F32),
                        pltpu.VMEM((seq, Q_TILE), BF16),
                        pltpu.VMEM((V_DIM, seq), BF16)],
        compiler_params=pltpu.CompilerParams(
            dimension_semantics=("parallel", "parallel"), vmem_limit_bytes=VMEM_LIMIT_BYTES),
        name="attn",
    )(lam, q, k, v, onw)


def _ffn(x2, a, b, layer, params, seq):
    n = x2.shape[0]
    tm = TOKEN_TILE
    tok = lambda w: pl.BlockSpec((tm, w), lambda i: (i, 0))
    return pl.pallas_call(
        functools.partial(_ffn_kernel, tiles_per_seq=seq // tm),
        grid=(n // tm,),
        in_specs=[tok(D_MODEL), tok(A_WIDTH), tok(B_WIDTH)] + [_layer_spec(p, layer) for p in params],
        out_specs=tok(D_MODEL),
        out_shape=jax.ShapeDtypeStruct((n, D_MODEL), F32),
        scratch_shapes=[pltpu.VMEM((tm + SUBLANES, D_FF), F32), pltpu.VMEM((SUBLANES, D_FF), F32)],
        compiler_params=pltpu.CompilerParams(
            dimension_semantics=("arbitrary",), vmem_limit_bytes=VMEM_LIMIT_BYTES),
        name="ffn",
    )(x2, a, b, *params)


def kernel(x, norm_attn_w, w_in, gmlp_v_norm_w, spatial_w, spatial_b, gmlp_out_norm_w, q_norm_w, k_norm_w, lambda_q1, lambda_k1, lambda_q2, lambda_k2, diff_out_norm_w, w_out, norm_ffn_w, w_gate, w_up, conv_w, conv_b, w_down):
    bsz, seq, _ = x.shape
    depth = w_in.shape[0]
    assert seq % MIXER_TILE == 0 and seq % TOKEN_TILE == 0 and seq % Q_TILE == 0
    n = bsz * seq
    row = lambda p: p.reshape(depth, 1, -1)
    lambda_init = [0.8 - 0.6 * math.exp(-0.3 * (layer + 1)) for layer in range(depth)]
    lam = (jnp.exp(jnp.sum(lambda_q1 * lambda_k1, axis=-1)) - jnp.exp(jnp.sum(lambda_q2 * lambda_k2, axis=-1))
           + jnp.asarray(lambda_init, F32))
    mixer_params = (row(norm_attn_w), w_in.astype(BF16), row(gmlp_v_norm_w), spatial_w,
                    jnp.swapaxes(spatial_b, 1, 2), row(gmlp_out_norm_w),
                    row(jnp.tile(q_norm_w, (1, 2))), row(jnp.tile(k_norm_w, (1, 2))))
    attn_norm = row(diff_out_norm_w)
    ffn_params = (w_out.astype(BF16), row(norm_ffn_w), w_gate.astype(BF16), w_up.astype(BF16),
                  conv_w, row(conv_b), w_down.astype(BF16))
    x2 = x.reshape(n, D_MODEL)
    for layer in range(depth):
        a, q, k, v = _mixer_in(x2, layer, mixer_params)
        b = _attn(lam, q.reshape(bsz, seq, B_WIDTH), k.reshape(bsz, seq, B_WIDTH),
                  v.reshape(bsz, seq, B_WIDTH), attn_norm, layer, 1.0 - lambda_init[layer])
        x2 = _ffn(x2, a, b.reshape(n, B_WIDTH), layer, ffn_params, seq)
    return x2.reshape(bsz, seq, D_MODEL)
```

```python
import functools
import math

import jax
import jax.numpy as jnp
from jax import lax
from jax.experimental import pallas as pl
from jax.experimental.pallas import tpu as pltpu

D_MODEL = 1024
CHUNK = 64
A_WIDTH = 512
B_WIDTH = 512
A_HEADS = 4
A_HEAD_DIM = 128
SPATIAL_CHUNK = 128
DIFF_HEADS = 4
DIFF_HEAD_DIM = 64
V_DIM = 2 * DIFF_HEAD_DIM
D_FF = 2816
CONV_WIDTH = 3
EPS = 1e-6
LOG2_E = math.log2(math.e)
IN_WIDTH = 2 * A_WIDTH + 3 * B_WIDTH

LANES = 128
SUBLANES = 8
VMEM_LIMIT_BYTES = 56 * 1024 * 1024

MIXER_TILE = 1024
TOKEN_TILE = 512
Q_TILE = 256

BF16 = jnp.bfloat16
F32 = jnp.float32


def _rms(x, w):
    return x * lax.rsqrt(jnp.mean(x * x, axis=-1, keepdims=True) + EPS) * w


def _gelu(x):
    return 0.5 * x * (1.0 + lax.erf(x * math.sqrt(0.5)))


def _half_norm(x, w, lo):
    sq = x * x
    s_all = jnp.sum(sq, axis=-1, keepdims=True)
    s_lo = jnp.sum(jnp.where(lo, sq, 0.0), axis=-1, keepdims=True)
    ms = jnp.where(lo, s_lo, s_all - s_lo) * (1.0 / DIFF_HEAD_DIM)
    return x * lax.rsqrt(ms + EPS) * w


def _mixer_in_kernel(x_ref, nw_ref, win_ref, vnw_ref, sw_ref, sbt_ref, onw_ref, qnw_ref, knw_ref,
                     a_ref, q_ref, k_ref, v_ref):
    tm = x_ref.shape[0]
    n_chunks = tm // SPATIAL_CHUNK
    h = _rms(x_ref[...], nw_ref[...]).astype(BF16)
    proj = jnp.dot(h, win_ref[...], preferred_element_type=F32)

    row = lax.broadcasted_iota(jnp.int32, (SPATIAL_CHUNK, SPATIAL_CHUNK), 0)
    col = lax.broadcasted_iota(jnp.int32, (SPATIAL_CHUNK, SPATIAL_CHUNK), 1)
    w_mask = (col // CHUNK) <= (row // CHUNK)
    for hd in range(A_HEADS):
        cs = slice(hd * A_HEAD_DIM, (hd + 1) * A_HEAD_DIM)
        u = _gelu(proj[:, hd * A_HEAD_DIM:(hd + 1) * A_HEAD_DIM])
        vv = _gelu(proj[:, A_WIDTH + hd * A_HEAD_DIM:A_WIDTH + (hd + 1) * A_HEAD_DIM])
        vn = _rms(vv, vnw_ref[:, cs]).astype(BF16)
        w_s = jnp.where(w_mask, sw_ref[hd], 0.0).astype(BF16)
        rhs = jnp.concatenate(
            [vn[c * SPATIAL_CHUNK:(c + 1) * SPATIAL_CHUNK] for c in range(n_chunks)], axis=1)
        mixed = jnp.dot(w_s, rhs, preferred_element_type=F32)
        b_col = sbt_ref[:, hd:hd + 1]
        mixed = jnp.concatenate(
            [mixed[:, c * SPATIAL_CHUNK:(c + 1) * SPATIAL_CHUNK] + b_col for c in range(n_chunks)],
            axis=0)
        a_ref[:, cs] = _rms(u * mixed, onw_ref[:, cs]).astype(BF16)

    lo = lax.broadcasted_iota(jnp.int32, (1, LANES), 1) < DIFF_HEAD_DIM
    scale = DIFF_HEAD_DIM ** -0.5 * LOG2_E
    for hd in range(DIFF_HEADS):
        cs = slice(hd * V_DIM, (hd + 1) * V_DIM)
        qh = proj[:, 2 * A_WIDTH + hd * V_DIM:2 * A_WIDTH + (hd + 1) * V_DIM]
        kh = proj[:, 2 * A_WIDTH + B_WIDTH + hd * V_DIM:2 * A_WIDTH + B_WIDTH + (hd + 1) * V_DIM]
        q_ref[:, cs] = (_half_norm(qh, qnw_ref[...], lo) * scale).astype(BF16)
        k_ref[:, cs] = _half_norm(kh, knw_ref[...], lo).astype(BF16)
    v_ref[...] = proj[:, 2 * A_WIDTH + 2 * B_WIDTH:].astype(BF16)


def _interleave(tasks_a, tasks_b):
    na, nb = len(tasks_a), len(tasks_b)
    ia = ib = 0
    while ia < na or ib < nb:
        if ib >= nb or (ia < na and (ia + 1) * nb <= (ib + 1) * na):
            tasks_a[ia]()
            ia += 1
        else:
            tasks_b[ib]()
            ib += 1


def _attn_kernel(lam_ref, q_ref, k_ref, v_ref, onw_ref, o_ref, s_ref, a_ref, vt_ref, *, layer, out_scale):
    seq = q_ref.shape[1]
    n_tiles = seq // Q_TILE
    lam = lam_ref[layer]
    lane = lax.broadcasted_iota(jnp.int32, (1, LANES), 1)
    keeps = (lane < DIFF_HEAD_DIM, lane >= DIFF_HEAD_DIM)
    key = lax.broadcasted_iota(jnp.int32, (Q_TILE, Q_TILE), 0)
    qry = lax.broadcasted_iota(jnp.int32, (Q_TILE, Q_TILE), 1)
    diag_mask = (key // CHUNK) <= (qry // CHUNK)
    qk_dims = (((1,), (1,)), ((), ()))

    def rows(u):
        return slice(u * Q_TILE, (u + 1) * Q_TILE)

    vt_ref[...] = v_ref[0].T

    def score_tasks(t, st):
        par = t % 2
        q = q_ref[0, rows(t), :]
        zero = jnp.zeros_like(q)
        q_subs = [jnp.where(keep, q, zero) for keep in keeps]

        def task(sub, u):
            def run():
                s = lax.dot_general(k_ref[0, rows(u), :], q_subs[sub], qk_dims, preferred_element_type=F32)
                if u == t:
                    s = jnp.where(diag_mask, s, -jnp.inf)
                s_ref[par, sub, rows(u), :] = s
                mx = jnp.max(s.reshape(Q_TILE // SUBLANES, SUBLANES, Q_TILE), axis=0)
                st["mx"][sub] = mx if u == 0 else jnp.maximum(st["mx"][sub], mx)
            return run
        return [task(sub, u) for u in range(t + 1) for sub in range(2)]

    def softmax_tasks(t, st):
        par = t % 2
        n = t + 1

        def exp_task(sub, u):
            def run():
                if u == 0:
                    st["m"][sub] = jnp.max(st["mx"][sub], axis=0, keepdims=True)
                p = jnp.exp2(s_ref[par, sub, rows(u), :] - st["m"][sub])
                s_ref[par, sub, rows(u), :] = p
                ls = jnp.sum(p.reshape(Q_TILE // SUBLANES, SUBLANES, Q_TILE), axis=0)
                st["ls"][sub] = ls if u == 0 else st["ls"][sub] + ls
                if u == n - 1:
                    st["l"][sub] = jnp.sum(st["ls"][sub], axis=0, keepdims=True)
            return run

        def mix_task(u):
            def run():
                if u == 0:
                    st["r"] = lam * st["l"][0] / st["l"][1]
                a_ref[rows(u), :] = (s_ref[par, 0, rows(u), :]
                                     - st["r"] * s_ref[par, 1, rows(u), :]).astype(BF16)
            return run

        return ([exp_task(sub, u) for u in range(n) for sub in range(2)]
                + [mix_task(u) for u in range(n)])

    def new_stats():
        return {"mx": [None, None], "m": [None, None], "ls": [None, None], "l": [None, None], "r": None}

    st_next = new_stats()
    for task in score_tasks(0, st_next):
        task()
    for t in range(n_tiles):
        n = t + 1
        st = st_next
        st_next = new_stats()
        _interleave(score_tasks(t + 1, st_next) if t + 1 < n_tiles else [], softmax_tasks(t, st))
        ot = jnp.dot(vt_ref[:, 0:n * Q_TILE], a_ref[0:n * Q_TILE, :], preferred_element_type=F32)
        ot = ot * (1.0 / st["l"][0])
        ot = ot * lax.rsqrt(jnp.mean(ot * ot, axis=0, keepdims=True) + EPS)
        o_ref[0, rows(t), :] = (ot.T * onw_ref[...] * out_scale).astype(BF16)


def _ffn_kernel(x_ref, a_ref, b_ref, wout_ref, nw_ref, wg_ref, wu_ref, cw_ref, cb_ref, wd_ref,
                o_ref, gs_ref, carry_ref, *, tiles_per_seq):
    tm = x_ref.shape[0]
    i = pl.program_id(0)

    @pl.when(i % tiles_per_seq == 0)
    def _():
        carry_ref[...] = jnp.zeros_like(carry_ref)

    half = tm // 2
    hs = []
    for r0 in (0, half):
        rs = slice(r0, r0 + half)
        x1 = (x_ref[rs, :]
              + jnp.dot(a_ref[rs, :], wout_ref[:A_WIDTH, :], preferred_element_type=F32)
              + jnp.dot(b_ref[rs, :], wout_ref[A_WIDTH:, :], preferred_element_type=F32))
        o_ref[rs, :] = x1
        hs.append(_rms(x1, nw_ref[...]).astype(BF16))
    gs_ref[0:SUBLANES, :] = carry_ref[...]
    for r0, h in zip((0, half), hs):
        gs_ref[SUBLANES + r0:SUBLANES + r0 + half, :] = jnp.dot(h, wg_ref[...], preferred_element_type=F32)
    carry_ref[...] = gs_ref[tm:tm + SUBLANES, :]
    g = gs_ref[SUBLANES:SUBLANES + tm, :]
    g1 = gs_ref[SUBLANES - 1:SUBLANES - 1 + tm, :]
    g2 = gs_ref[SUBLANES - 2:SUBLANES - 2 + tm, :]
    conv = cb_ref[...] + cw_ref[0:1, :] * g2 + cw_ref[1:2, :] * g1 + cw_ref[2:3, :] * g
    up = jnp.dot(jnp.concatenate(hs, axis=0), wu_ref[...], preferred_element_type=F32)
    f = (conv * jax.nn.sigmoid(conv) * up).astype(BF16)
    o_ref[...] += jnp.dot(f, wd_ref[...], preferred_element_type=F32)


def _layer_spec(arr, layer):
    nd = arr.ndim - 1
    return pl.BlockSpec((None,) + arr.shape[1:], lambda *_: (layer,) + (0,) * nd,
                        pipeline_mode=pl.Buffered(1))


def _mixer_in(x2, layer, params):
    n = x2.shape[0]
    tm = MIXER_TILE
    tok = lambda w: pl.BlockSpec((tm, w), lambda i: (i, 0))
    out = jax.ShapeDtypeStruct((n, B_WIDTH), BF16)
    return pl.pallas_call(
        _mixer_in_kernel,
        grid=(n // tm,),
        in_specs=[tok(D_MODEL)] + [_layer_spec(p, layer) for p in params],
        out_specs=[tok(A_WIDTH), tok(B_WIDTH), tok(B_WIDTH), tok(B_WIDTH)],
        out_shape=[out, out, out, out],
        compiler_params=pltpu.CompilerParams(
            dimension_semantics=("parallel",), vmem_limit_bytes=VMEM_LIMIT_BYTES),
        name="mixer_in",
    )(x2, *params)


def _attn(lam, q, k, v, onw, layer, out_scale):
    bsz, seq, _ = q.shape
    spec = pl.BlockSpec((1, seq, V_DIM), lambda b, h: (b, 0, h))
    return pl.pallas_call(
        functools.partial(_attn_kernel, layer=layer, out_scale=out_scale),
        grid=(bsz, DIFF_HEADS),
        in_specs=[pl.BlockSpec(memory_space=pltpu.SMEM), spec, spec, spec, _layer_spec(onw, layer)],
        out_specs=spec,
        out_shape=jax.ShapeDtypeStruct((bsz, seq, B_WIDTH), BF16),
        scratch_shapes=[pltpu.VMEM((2, 2, seq, Q_TILE), F32),
                        pltpu.VMEM((seq, Q_TILE), BF16),
                        pltpu.VMEM((V_DIM, seq), BF16)],
        compiler_params=pltpu.CompilerParams(
            dimension_semantics=("parallel", "parallel"), vmem_limit_bytes=VMEM_LIMIT_BYTES),
        name="attn",
    )(lam, q, k, v, onw)


def _ffn(x2, a, b, layer, params, seq):
    n = x2.shape[0]
    tm = TOKEN_TILE
    tok = lambda w: pl.BlockSpec((tm, w), lambda i: (i, 0))
    return pl.pallas_call(
        functools.partial(_ffn_kernel, tiles_per_seq=seq // tm),
        grid=(n // tm,),
        in_specs=[tok(D_MODEL), tok(A_WIDTH), tok(B_WIDTH)] + [_layer_spec(p, layer) for p in params],
        out_specs=tok(D_MODEL),
        out_shape=jax.ShapeDtypeStruct((n, D_MODEL), F32),
        scratch_shapes=[pltpu.VMEM((tm + SUBLANES, D_FF), F32), pltpu.VMEM((SUBLANES, D_FF), F32)],
        compiler_params=pltpu.CompilerParams(
            dimension_semantics=("arbitrary",), vmem_limit_bytes=VMEM_LIMIT_BYTES),
        name="ffn",
    )(x2, a, b, *params)


def kernel(x, norm_attn_w, w_in, gmlp_v_norm_w, spatial_w, spatial_b, gmlp_out_norm_w, q_norm_w, k_norm_w, lambda_q1, lambda_k1, lambda_q2, lambda_k2, diff_out_norm_w, w_out, norm_ffn_w, w_gate, w_up, conv_w, conv_b, w_down):
    bsz, seq, _ = x.shape
    depth = w_in.shape[0]
    assert seq % MIXER_TILE == 0 and seq % TOKEN_TILE == 0 and seq % Q_TILE == 0
    n = bsz * seq
    row = lambda p: p.reshape(depth, 1, -1)
    lambda_init = [0.8 - 0.6 * math.exp(-0.3 * (layer + 1)) for layer in range(depth)]
    lam = (jnp.exp(jnp.sum(lambda_q1 * lambda_k1, axis=-1)) - jnp.exp(jnp.sum(lambda_q2 * lambda_k2, axis=-1))
           + jnp.asarray(lambda_init, F32))
    mixer_params = (row(norm_attn_w), w_in.astype(BF16), row(gmlp_v_norm_w), spatial_w,
                    jnp.swapaxes(spatial_b, 1, 2), row(gmlp_out_norm_w),
                    row(jnp.tile(q_norm_w, (1, 2))), row(jnp.tile(k_norm_w, (1, 2))))
    attn_norm = row(diff_out_norm_w)
    ffn_params = (w_out.astype(BF16), row(norm_ffn_w), w_gate.astype(BF16), w_up.astype(BF16),
                  conv_w, row(conv_b), w_down.astype(BF16))
    x2 = x.reshape(n, D_MODEL)
    for layer in range(depth):
        a, q, k, v = _mixer_in(x2, layer, mixer_params)
        b = _attn(lam, q.reshape(bsz, seq, B_WIDTH), k.reshape(bsz, seq, B_WIDTH),
                  v.reshape(bsz, seq, B_WIDTH), attn_norm, layer, 1.0 - lambda_init[layer])
        x2 = _ffn(x2, a, b.reshape(n, B_WIDTH), layer, ffn_params, seq)
    return x2.reshape(bsz, seq, D_MODEL)
```

```python
import functools
import math

import jax
import jax.numpy as jnp
from jax import lax
from jax.experimental import pallas as pl
from jax.experimental.pallas import tpu as pltpu

D_MODEL = 1024
CHUNK = 64
A_WIDTH = 512
B_WIDTH = 512
A_HEADS = 4
A_HEAD_DIM = 128
SPATIAL_CHUNK = 128
DIFF_HEADS = 4
DIFF_HEAD_DIM = 64
V_DIM = 2 * DIFF_HEAD_DIM
D_FF = 2816
CONV_WIDTH = 3
EPS = 1e-6
LOG2_E = math.log2(math.e)
IN_WIDTH = 2 * A_WIDTH + 3 * B_WIDTH

LANES = 128
SUBLANES = 8
VMEM_LIMIT_BYTES = 56 * 1024 * 1024

MIXER_TILE = 1024
TOKEN_TILE = 512
Q_TILE = 512
HEADS_PER_STEP = 2

BF16 = jnp.bfloat16
F32 = jnp.float32


def _rms(x, w):
    return x * lax.rsqrt(jnp.mean(x * x, axis=-1, keepdims=True) + EPS) * w


def _gelu(x):
    return 0.5 * x * (1.0 + lax.erf(x * math.sqrt(0.5)))


def _half_norm(x, w, lo):
    sq = x * x
    s_all = jnp.sum(sq, axis=-1, keepdims=True)
    s_lo = jnp.sum(jnp.where(lo, sq, 0.0), axis=-1, keepdims=True)
    ms = jnp.where(lo, s_lo, s_all - s_lo) * (1.0 / DIFF_HEAD_DIM)
    return x * lax.rsqrt(ms + EPS) * w


def _mixer_in_kernel(x_ref, nw_ref, win_ref, vnw_ref, sw_ref, sbt_ref, onw_ref, qnw_ref, knw_ref,
                     a_ref, q_ref, k_ref, v_ref):
    tm = x_ref.shape[0]
    n_chunks = tm // SPATIAL_CHUNK
    h = _rms(x_ref[...], nw_ref[...]).astype(BF16)
    proj = jnp.dot(h, win_ref[...], preferred_element_type=F32)

    row = lax.broadcasted_iota(jnp.int32, (SPATIAL_CHUNK, SPATIAL_CHUNK), 0)
    col = lax.broadcasted_iota(jnp.int32, (SPATIAL_CHUNK, SPATIAL_CHUNK), 1)
    w_mask = (col // CHUNK) <= (row // CHUNK)
    for hd in range(A_HEADS):
        cs = slice(hd * A_HEAD_DIM, (hd + 1) * A_HEAD_DIM)
        u = _gelu(proj[:, hd * A_HEAD_DIM:(hd + 1) * A_HEAD_DIM])
        vv = _gelu(proj[:, A_WIDTH + hd * A_HEAD_DIM:A_WIDTH + (hd + 1) * A_HEAD_DIM])
        vn = _rms(vv, vnw_ref[:, cs]).astype(BF16)
        w_s = jnp.where(w_mask, sw_ref[hd], 0.0).astype(BF16)
        rhs = jnp.concatenate(
            [vn[c * SPATIAL_CHUNK:(c + 1) * SPATIAL_CHUNK] for c in range(n_chunks)], axis=1)
        mixed = jnp.dot(w_s, rhs, preferred_element_type=F32)
        b_col = sbt_ref[:, hd:hd + 1]
        mixed = jnp.concatenate(
            [mixed[:, c * SPATIAL_CHUNK:(c + 1) * SPATIAL_CHUNK] + b_col for c in range(n_chunks)],
            axis=0)
        a_ref[:, cs] = _rms(u * mixed, onw_ref[:, cs]).astype(BF16)

    lo = lax.broadcasted_iota(jnp.int32, (1, LANES), 1) < DIFF_HEAD_DIM
    scale = DIFF_HEAD_DIM ** -0.5 * LOG2_E
    for hd in range(DIFF_HEADS):
        cs = slice(hd * V_DIM, (hd + 1) * V_DIM)
        qh = proj[:, 2 * A_WIDTH + hd * V_DIM:2 * A_WIDTH + (hd + 1) * V_DIM]
        kh = proj[:, 2 * A_WIDTH + B_WIDTH + hd * V_DIM:2 * A_WIDTH + B_WIDTH + (hd + 1) * V_DIM]
        q_ref[:, cs] = (_half_norm(qh, qnw_ref[...], lo) * scale).astype(BF16)
        k_ref[:, cs] = _half_norm(kh, knw_ref[...], lo).astype(BF16)
    v_ref[...] = proj[:, 2 * A_WIDTH + 2 * B_WIDTH:].astype(BF16)


def _interleave(tasks_a, tasks_b):
    na, nb = len(tasks_a), len(tasks_b)
    ia = ib = 0
    while ia < na or ib < nb:
        if ib >= nb or (ia < na and (ia + 1) * nb <= (ib + 1) * na):
            tasks_a[ia]()
            ia += 1
        else:
            tasks_b[ib]()
            ib += 1


def _attn_kernel(lam_ref, q_ref, k_ref, v_ref, onw_ref, o_ref, s_all, a_all, vt_all, *, layer, out_scale):
    seq = q_ref.shape[1]
    n_tiles = seq // Q_TILE
    lam = lam_ref[layer]
    lane = lax.broadcasted_iota(jnp.int32, (1, LANES), 1)
    keeps = (lane < DIFF_HEAD_DIM, lane >= DIFF_HEAD_DIM)
    key = lax.broadcasted_iota(jnp.int32, (Q_TILE, Q_TILE), 0)
    qry = lax.broadcasted_iota(jnp.int32, (Q_TILE, Q_TILE), 1)
    diag_mask = (key // CHUNK) <= (qry // CHUNK)
    qk_dims = (((1,), (1,)), ((), ()))

    def rows(u):
        return slice(u * Q_TILE, (u + 1) * Q_TILE)

    for hd in range(HEADS_PER_STEP):
        _attn_head(hd, lam, keeps, diag_mask, qk_dims, rows, n_tiles, q_ref, k_ref, v_ref, onw_ref, o_ref,
                   s_all.at[hd], a_all.at[hd], vt_all.at[hd], out_scale)


def _attn_head(hd, lam, keeps, diag_mask, qk_dims, rows, n_tiles, q_ref, k_ref, v_ref, onw_ref, o_ref,
               s_ref, a_ref, vt_ref, out_scale):
    hs = slice(hd * V_DIM, (hd + 1) * V_DIM)
    vt_ref[...] = v_ref[0, :, hs].T

    def score_tasks(t, st):
        par = t % 2
        q = q_ref[0, rows(t), hs]
        zero = jnp.zeros_like(q)
        q_subs = [jnp.where(keep, q, zero) for keep in keeps]

        def task(sub, u):
            def run():
                s = lax.dot_general(k_ref[0, rows(u), hs], q_subs[sub], qk_dims, preferred_element_type=F32)
                if u == t:
                    s = jnp.where(diag_mask, s, -jnp.inf)
                s_ref[par, sub, rows(u), :] = s
                mx = jnp.max(s.reshape(Q_TILE // SUBLANES, SUBLANES, Q_TILE), axis=0)
                st["mx"][sub] = mx if u == 0 else jnp.maximum(st["mx"][sub], mx)
            return run
        return [task(sub, u) for u in range(t + 1) for sub in range(2)]

    def softmax_tasks(t, st):
        par = t % 2
        n = t + 1

        def exp_task(sub, u):
            def run():
                if u == 0:
                    st["m"][sub] = jnp.max(st["mx"][sub], axis=0, keepdims=True)
                p = jnp.exp2(s_ref[par, sub, rows(u), :] - st["m"][sub])
                s_ref[par, sub, rows(u), :] = p
                ls = jnp.sum(p.reshape(Q_TILE // SUBLANES, SUBLANES, Q_TILE), axis=0)
                st["ls"][sub] = ls if u == 0 else st["ls"][sub] + ls
                if u == n - 1:
                    st["l"][sub] = jnp.sum(st["ls"][sub], axis=0, keepdims=True)
            return run

        def mix_task(u):
            def run():
                if u == 0:
                    st["r"] = lam * st["l"][0] / st["l"][1]
                a_ref[rows(u), :] = (s_ref[par, 0, rows(u), :]
                                     - st["r"] * s_ref[par, 1, rows(u), :]).astype(BF16)
            return run

        return ([exp_task(sub, u) for u in range(n) for sub in range(2)]
                + [mix_task(u) for u in range(n)])

    def new_stats():
        return {"mx": [None, None], "m": [None, None], "ls": [None, None], "l": [None, None], "r": None}

    st_next = new_stats()
    for task in score_tasks(0, st_next):
        task()
    for t in range(n_tiles):
        n = t + 1
        st = st_next
        st_next = new_stats()
        _interleave(score_tasks(t + 1, st_next) if t + 1 < n_tiles else [], softmax_tasks(t, st))
        ot = jnp.dot(vt_ref[:, 0:n * Q_TILE], a_ref[0:n * Q_TILE, :], preferred_element_type=F32)
        ot = ot * (1.0 / st["l"][0])
        ot = ot * lax.rsqrt(jnp.mean(ot * ot, axis=0, keepdims=True) + EPS)
        o_ref[0, rows(t), hs] = (ot.T * onw_ref[...] * out_scale).astype(BF16)


def _ffn_kernel(x_ref, a_ref, b_ref, wout_ref, nw_ref, wg_ref, wu_ref, cw_ref, cb_ref, wd_ref,
                o_ref, gs_ref, carry_ref, *, tiles_per_seq):
    tm = x_ref.shape[0]
    i = pl.program_id(0)

    @pl.when(i % tiles_per_seq == 0)
    def _():
        carry_ref[...] = jnp.zeros_like(carry_ref)

    half = tm // 2
    hs = []
    for r0 in (0, half):
        rs = slice(r0, r0 + half)
        x1 = (x_ref[rs, :]
              + jnp.dot(a_ref[rs, :], wout_ref[:A_WIDTH, :], preferred_element_type=F32)
              + jnp.dot(b_ref[rs, :], wout_ref[A_WIDTH:, :], preferred_element_type=F32))
        o_ref[rs, :] = x1
        hs.append(_rms(x1, nw_ref[...]).astype(BF16))
    gs_ref[0:SUBLANES, :] = carry_ref[...]
    for r0, h in zip((0, half), hs):
        gs_ref[SUBLANES + r0:SUBLANES + r0 + half, :] = jnp.dot(h, wg_ref[...], preferred_element_type=F32)
    carry_ref[...] = gs_ref[tm:tm + SUBLANES, :]
    g = gs_ref[SUBLANES:SUBLANES + tm, :]
    g1 = gs_ref[SUBLANES - 1:SUBLANES - 1 + tm, :]
    g2 = gs_ref[SUBLANES - 2:SUBLANES - 2 + tm, :]
    conv = cb_ref[...] + cw_ref[0:1, :] * g2 + cw_ref[1:2, :] * g1 + cw_ref[2:3, :] * g
    up = jnp.dot(jnp.concatenate(hs, axis=0), wu_ref[...], preferred_element_type=F32)
    f = (conv * jax.nn.sigmoid(conv) * up).astype(BF16)
    o_ref[...] += jnp.dot(f, wd_ref[...], preferred_element_type=F32)


def _layer_spec(arr, layer):
    nd = arr.ndim - 1
    return pl.BlockSpec((None,) + arr.shape[1:], lambda *_: (layer,) + (0,) * nd,
                        pipeline_mode=pl.Buffered(1))


def _mixer_in(x2, layer, params):
    n = x2.shape[0]
    tm = MIXER_TILE
    tok = lambda w: pl.BlockSpec((tm, w), lambda i: (i, 0))
    out = jax.ShapeDtypeStruct((n, B_WIDTH), BF16)
    return pl.pallas_call(
        _mixer_in_kernel,
        grid=(n // tm,),
        in_specs=[tok(D_MODEL)] + [_layer_spec(p, layer) for p in params],
        out_specs=[tok(A_WIDTH), tok(B_WIDTH), tok(B_WIDTH), tok(B_WIDTH)],
        out_shape=[out, out, out, out],
        compiler_params=pltpu.CompilerParams(
            dimension_semantics=("parallel",), vmem_limit_bytes=VMEM_LIMIT_BYTES),
        name="mixer_in",
    )(x2, *params)


def _attn(lam, q, k, v, onw, layer, out_scale):
    bsz, seq, _ = q.shape
    spec = pl.BlockSpec((1, seq, HEADS_PER_STEP * V_DIM), lambda b, h: (b, 0, h))
    return pl.pallas_call(
        functools.partial(_attn_kernel, layer=layer, out_scale=out_scale),
        grid=(bsz, DIFF_HEADS // HEADS_PER_STEP),
        in_specs=[pl.BlockSpec(memory_space=pltpu.SMEM), spec, spec, spec, _layer_spec(onw, layer)],
        out_specs=spec,
        out_shape=jax.ShapeDtypeStruct((bsz, seq, B_WIDTH), BF16),
        scratch_shapes=[pltpu.VMEM((HEADS_PER_STEP, 2, 2, seq, Q_TILE), F32),
                        pltpu.VMEM((HEADS_PER_STEP, seq, Q_TILE), BF16),
                        pltpu.VMEM((HEADS_PER_STEP, V_DIM, seq), BF16)],
        compiler_params=pltpu.CompilerParams(
            dimension_semantics=("parallel", "parallel"), vmem_limit_bytes=VMEM_LIMIT_BYTES),
        name="attn",
    )(lam, q, k, v, onw)


def _ffn(x2, a, b, layer, params, seq):
    n = x2.shape[0]
    tm = TOKEN_TILE
    tok = lambda w: pl.BlockSpec((tm, w), lambda i: (i, 0))
    return pl.pallas_call(
        functools.partial(_ffn_kernel, tiles_per_seq=seq // tm),
        grid=(n // tm,),
        in_specs=[tok(D_MODEL), tok(A_WIDTH), tok(B_WIDTH)] + [_layer_spec(p, layer) for p in params],
        out_specs=tok(D_MODEL),
        out_shape=jax.ShapeDtypeStruct((n, D_MODEL), F32),
        scratch_shapes=[pltpu.VMEM((tm + SUBLANES, D_FF), F32), pltpu.VMEM((SUBLANES, D_FF), F32)],
        compiler_params=pltpu.CompilerParams(
            dimension_semantics=("arbitrary",), vmem_limit_bytes=VMEM_LIMIT_BYTES),
        name="ffn",
    )(x2, a, b, *params)


def kernel(x, norm_attn_w, w_in, gmlp_v_norm_w, spatial_w, spatial_b, gmlp_out_norm_w, q_norm_w, k_norm_w, lambda_q1, lambda_k1, lambda_q2, lambda_k2, diff_out_norm_w, w_out, norm_ffn_w, w_gate, w_up, conv_w, conv_b, w_down):
    bsz, seq, _ = x.shape
    depth = w_in.shape[0]
    assert seq % MIXER_TILE == 0 and seq % TOKEN_TILE == 0 and seq % Q_TILE == 0
    n = bsz * seq
    row = lambda p: p.reshape(depth, 1, -1)
    lambda_init = [0.8 - 0.6 * math.exp(-0.3 * (layer + 1)) for layer in range(depth)]
    lam = (jnp.exp(jnp.sum(lambda_q1 * lambda_k1, axis=-1)) - jnp.exp(jnp.sum(lambda_q2 * lambda_k2, axis=-1))
           + jnp.asarray(lambda_init, F32))
    mixer_params = (row(norm_attn_w), w_in.astype(BF16), row(gmlp_v_norm_w), spatial_w,
                    jnp.swapaxes(spatial_b, 1, 2), row(gmlp_out_norm_w),
                    row(jnp.tile(q_norm_w, (1, 2))), row(jnp.tile(k_norm_w, (1, 2))))
    attn_norm = row(diff_out_norm_w)
    ffn_params = (w_out.astype(BF16), row(norm_ffn_w), w_gate.astype(BF16), w_up.astype(BF16),
                  conv_w, row(conv_b), w_down.astype(BF16))
    x2 = x.reshape(n, D_MODEL)
    for layer in range(depth):
        a, q, k, v = _mixer_in(x2, layer, mixer_params)
        b = _attn(lam, q.reshape(bsz, seq, B_WIDTH), k.reshape(bsz, seq, B_WIDTH),
                  v.reshape(bsz, seq, B_WIDTH), attn_norm, layer, 1.0 - lambda_init[layer])
        x2 = _ffn(x2, a, b.reshape(n, B_WIDTH), layer, ffn_params, seq)
    return x2.reshape(bsz, seq, D_MODEL)
```
